```python
import math
import jax, jax.numpy as jnp
from jax import lax
import numpy as np

D_MODEL = 1024
BATCH = 8
SEQ = 4096
DEPTH = 4

PLE_DIM = 256
CHUNK = 128
EPS = 1e-6
N_BRANCH = 2
M_HEADS = 4
M_QK = 128
M_V = 256
M_INNER = M_HEADS * M_V
S_INNER = 2 * D_MODEL
S_HEADDIM = 64
S_HEADS = S_INNER // S_HEADDIM
S_GROUPS = 4
S_STATE = 128
S_CONV = 4
S_CONV_DIM = S_INNER + 2 * S_GROUPS * S_STATE
D_FF = 2816
SPLIT_SIZES = (M_HEADS * M_QK, M_HEADS * M_QK, M_INNER, M_INNER, M_HEADS, M_HEADS,
               S_INNER, S_CONV_DIM, S_HEADS, N_BRANCH * D_MODEL)
D_IN = 2 * M_HEADS * M_QK + 2 * M_INNER + 2 * M_HEADS + S_INNER + S_CONV_DIM + S_HEADS + N_BRANCH * D_MODEL

kernel_name = "hybrid_mlstm_ssd_macaron_trunk"


def rmsnorm(x, g):
    xf = x.astype(jnp.float32)
    r = lax.rsqrt(jnp.mean(xf * xf, axis=-1, keepdims=True) + EPS)
    return (xf * r).astype(x.dtype) * g


def swiglu(u, w1, w3, w2):
    return (jax.nn.silu(u @ w1) * (u @ w3)) @ w2


def causal_dwconv(x, w, b):
    c = x.shape[-1]
    y = lax.conv_general_dilated(x, w[:, None, :].astype(x.dtype), window_strides=(1,),
                                 padding=[(S_CONV - 1, 0)], dimension_numbers=("NWC", "WIO", "NWC"),
                                 feature_group_count=c)
    return y + b


def mlstm_chunkwise(q, k, v, i_pre, f_pre):
    bsz, s_len, nh, dk = q.shape
    dv = v.shape[-1]
    L = CHUNK
    nc = s_len // L
    f32 = jnp.float32
    q = q.reshape(bsz, nc, L, nh, dk) * (dk ** -0.5)
    k = k.reshape(bsz, nc, L, nh, dk)
    v = v.reshape(bsz, nc, L, nh, dv)
    log_i = jnp.moveaxis(i_pre.astype(f32).reshape(bsz, nc, L, nh), 3, 2)
    log_f = jnp.moveaxis(jax.nn.log_sigmoid(f_pre.astype(f32)).reshape(bsz, nc, L, nh), 3, 2)
    bcum = jnp.cumsum(log_f, axis=-1)
    gtot = bcum[..., -1]
    causal = jnp.tril(jnp.ones((L, L), dtype=bool))
    dlog = jnp.where(causal, bcum[..., :, None] - bcum[..., None, :] + log_i[..., None, :], -jnp.inf)

    a = gtot[..., None] - bcum + log_i
    m_loc = jnp.max(a, axis=-1)
    wa = jnp.exp(a - m_loc[..., None]).astype(v.dtype)
    c_loc = jnp.einsum('bchs,bcshk,bcshv->bchkv', wa, k, v)
    n_loc = jnp.einsum('bchs,bcshk->bchk', wa, k)

    def step(carry, inp):
        c_st, n_st, m_st = carry
        cl, nl, ml, gc = inp
        m_new = jnp.maximum(gc + m_st, ml)
        sp = jnp.exp(gc + m_st - m_new)
        sl = jnp.exp(ml - m_new)
        c_new = sp[..., None, None] * c_st + sl[..., None, None] * cl
        n_new = sp[..., None] * n_st + sl[..., None] * nl
        return (c_new, n_new, m_new), (c_st, n_st, m_st)

    init = (jnp.zeros((bsz, nh, dk, dv), f32), jnp.zeros((bsz, nh, dk), f32), jnp.zeros((bsz, nh), f32))
    _, (c_start, n_start, m_start) = lax.scan(
        step, init, (jnp.moveaxis(c_loc, 1, 0), jnp.moveaxis(n_loc, 1, 0),
                     jnp.moveaxis(m_loc, 1, 0), jnp.moveaxis(gtot, 1, 0)))
    c_start = jnp.moveaxis(c_start, 0, 1)
    n_start = jnp.moveaxis(n_start, 0, 1)
    m_start = jnp.moveaxis(m_start, 0, 1)

    m_inter = bcum + m_start[..., None]
    m_t = jnp.maximum(jnp.max(dlog, axis=-1), m_inter)
    w_intra = jnp.exp(dlog - m_t[..., None])
    w_inter = jnp.moveaxis(jnp.exp(m_inter - m_t), 3, 2)
    sc = jnp.einsum('bcthk,bcshk->bchts', q, k) * w_intra.astype(q.dtype)
    num = (jnp.einsum('bchts,bcshv->bcthv', sc, v)
           + jnp.einsum('bcthk,bchkv->bcthv', q, c_start.astype(q.dtype)) * w_inter[..., None])
    den = (jnp.moveaxis(jnp.sum(sc.astype(f32), axis=-1), 3, 2)
           + w_inter * jnp.einsum('bcthk,bchk->bcth', q, n_start.astype(q.dtype)))
    floor = jnp.exp(-jnp.moveaxis(m_t, 3, 2))
    h = num / jnp.maximum(jnp.abs(den), floor)[..., None]
    return h.reshape(bsz, s_len, nh, dv).astype(v.dtype)


def ssd_chunked(x, dt, a_neg, bm, cm):
    bsz, s_len, nh, hp = x.shape
    ng, ns = bm.shape[-2:]
    ne = nh // ng
    L = CHUNK
    nc = s_len // L
    xw = (x * dt[..., None].astype(x.dtype)).reshape(bsz, nc, L, ng, ne, hp)
    acs = jnp.cumsum((dt * a_neg).reshape(bsz, nc, L, ng, ne), axis=2)
    ah = jnp.transpose(acs, (0, 1, 3, 4, 2))
    bc = bm.reshape(bsz, nc, L, ng, ns)
    cc = cm.reshape(bsz, nc, L, ng, ns)
    causal = jnp.tril(jnp.ones((L, L), dtype=bool))
    decay = jnp.exp(jnp.where(causal, ah[..., :, None] - ah[..., None, :], -jnp.inf))
    cb = jnp.einsum('bctgn,bcsgn->bcgts', cc, bc)
    y_intra = jnp.einsum('bcgts,bcgets,bcsgep->bctgep', cb, decay.astype(x.dtype), xw)

    w_end = jnp.exp(ah[..., -1:] - ah).astype(x.dtype)
    st_loc = jnp.einsum('bcges,bcsgn,bcsgep->bcgepn', w_end, bc, xw)
    g_chunk = jnp.exp(ah[..., -1])

    def step(st, inp):
        sl, gc = inp
        return gc[..., None, None] * st + sl, st

    _, s_start = lax.scan(step, jnp.zeros((bsz, ng, ne, hp, ns), jnp.float32),
                          (jnp.moveaxis(st_loc, 1, 0), jnp.moveaxis(g_chunk, 1, 0)))
    s_start = jnp.moveaxis(s_start, 0, 1)
    y_inter = (jnp.einsum('bctgn,bcgepn->bctgep', cc, s_start.astype(x.dtype))
               * jnp.exp(acs)[..., None].astype(x.dtype))
    return (y_intra + y_inter).reshape(bsz, s_len, nh, hp).astype(x.dtype)


def _dense(key, shape, fan_in):
    return jax.random.normal(key, shape, jnp.float32) * (fan_in ** -0.5)


def _gain(key, shape):
    return 1.0 + 0.02 * jax.random.normal(key, shape, jnp.float32)


def setup_inputs(seed: int = 0) -> dict:
    key = jax.random.key(seed)
    ks = jax.random.split(key, 32)
    nrm = jax.random.normal
    dt0 = jnp.exp(jax.random.uniform(ks[12], (DEPTH, S_HEADS), jnp.float32,
                                     minval=math.log(1e-3), maxval=math.log(1e-1)))
    return {
        "x": nrm(ks[0], (BATCH, SEQ, D_MODEL), jnp.float32),
        "p": nrm(ks[1], (DEPTH, BATCH, SEQ, PLE_DIM), jnp.float32),
        "g_ffn1": _gain(ks[2], (DEPTH, D_MODEL)),
        "w1_ffn1": _dense(ks[3], (DEPTH, D_MODEL, D_FF), D_MODEL),
        "w3_ffn1": _dense(ks[4], (DEPTH, D_MODEL, D_FF), D_MODEL),
        "w2_ffn1": _dense(ks[5], (DEPTH, D_FF, D_MODEL), D_FF),
        "g_mix": _gain(ks[6], (DEPTH, D_MODEL)),
        "w_in": _dense(ks[7], (DEPTH, D_MODEL, D_IN), D_MODEL),
        "b_igate": 0.1 * nrm(ks[8], (DEPTH, M_HEADS), jnp.float32),
        "b_fgate": jnp.linspace(3.0, 6.0, M_HEADS, dtype=jnp.float32)[None, :]
                   + 0.1 * nrm(ks[9], (DEPTH, M_HEADS), jnp.float32),
        "g_mlstm_head": _gain(ks[10], (DEPTH, M_HEADS, M_V)),
        "conv_w": _dense(ks[11], (DEPTH, S_CONV, S_CONV_DIM), S_CONV),
        "conv_b": 0.01 * nrm(ks[13], (DEPTH, S_CONV_DIM), jnp.float32),
        "dt_bias": dt0 + jnp.log(-jnp.expm1(-dt0)),
        "a_log": jnp.log(jax.random.uniform(ks[14], (DEPTH, S_HEADS), jnp.float32, minval=1.0, maxval=16.0)),
        "d_skip": 1.0 + 0.1 * nrm(ks[15], (DEPTH, S_HEADS), jnp.float32),
        "g_ssm_out": _gain(ks[16], (DEPTH, S_INNER)),
        "w_branch_a": _dense(ks[17], (DEPTH, M_INNER, D_MODEL), M_INNER),
        "w_branch_b": _dense(ks[18], (DEPTH, S_INNER, D_MODEL), S_INNER),
        "w_out": _dense(ks[19], (DEPTH, D_MODEL, D_MODEL), D_MODEL),
        "g_ffn2": _gain(ks[20], (DEPTH, D_MODEL)),
        "w1_ffn2": _dense(ks[21], (DEPTH, D_MODEL, D_FF), D_MODEL),
        "w3_ffn2": _dense(ks[22], (DEPTH, D_MODEL, D_FF), D_MODEL),
        "w2_ffn2": _dense(ks[23], (DEPTH, D_FF, D_MODEL), D_FF),
        "g_ple": _gain(ks[24], (DEPTH, D_MODEL)),
        "w_ple_gate": _dense(ks[25], (DEPTH, D_MODEL, D_MODEL), D_MODEL),
        "w_ple": _dense(ks[26], (DEPTH, PLE_DIM, D_MODEL), PLE_DIM),
        "g_final": _gain(ks[27], (D_MODEL,)),
    }


def reference(x, p, g_ffn1, w1_ffn1, w3_ffn1, w2_ffn1, g_mix, w_in, b_igate, b_fgate, g_mlstm_head,
              conv_w, conv_b, dt_bias, a_log, d_skip, g_ssm_out, w_branch_a, w_branch_b, w_out,
              g_ffn2, w1_ffn2, w3_ffn2, w2_ffn2, g_ple, w_ple_gate, w_ple, g_final):
    bsz, s_len, _ = x.shape
    split_at = [int(c) for c in np.cumsum(SPLIT_SIZES)[:-1]]
    h = x
    for i in range(DEPTH):
        h = h + 0.5 * swiglu(rmsnorm(h, g_ffn1[i]), w1_ffn1[i], w3_ffn1[i], w2_ffn1[i])

        u = rmsnorm(h, g_mix[i])
        proj = u @ w_in[i]
        q, k, v, o_pre, i_pre, f_pre, z, xbc, dt_raw, gate_pre = jnp.split(proj, split_at, axis=-1)

        ha = mlstm_chunkwise(q.reshape(bsz, s_len, M_HEADS, M_QK), k.reshape(bsz, s_len, M_HEADS, M_QK),
                             v.reshape(bsz, s_len, M_HEADS, M_V), i_pre + b_igate[i], f_pre + b_fgate[i])
        ha = rmsnorm(ha, g_mlstm_head[i]).reshape(bsz, s_len, M_INNER) * jax.nn.sigmoid(o_pre)

        xbc = jax.nn.silu(causal_dwconv(xbc, conv_w[i], conv_b[i]))
        xs, bm, cm = jnp.split(xbc, [S_INNER, S_INNER + S_GROUPS * S_STATE], axis=-1)
        xs = xs.reshape(bsz, s_len, S_HEADS, S_HEADDIM)
        dt = jax.nn.softplus(dt_raw.astype(jnp.float32) + dt_bias[i].astype(jnp.float32))
        a_neg = -jnp.exp(a_log[i].astype(jnp.float32))
        yb = ssd_chunked(xs, dt, a_neg, bm.reshape(bsz, s_len, S_GROUPS, S_STATE),
                         cm.reshape(bsz, s_len, S_GROUPS, S_STATE))
        yb = (yb + xs * d_skip[i][:, None]).reshape(bsz, s_len, S_INNER)
        yb = rmsnorm(yb * jax.nn.silu(z), g_ssm_out[i])

        g_a, g_b = jnp.split(jax.nn.sigmoid(gate_pre), N_BRANCH, axis=-1)
        merged = g_a * (ha @ w_branch_a[i]) + g_b * (yb @ w_branch_b[i])
        h = h + merged @ w_out[i]

        h = h + 0.5 * swiglu(rmsnorm(h, g_ffn2[i]), w1_ffn2[i], w3_ffn2[i], w2_ffn2[i])

        h = h + jax.nn.sigmoid(rmsnorm(h, g_ple[i]) @ w_ple_gate[i]) * (p[i] @ w_ple[i])
    return rmsnorm(h, g_final)
```

```python
import functools
import math

import numpy as np
import jax
import jax.numpy as jnp
from jax import lax
from jax.experimental import pallas as pl
from jax.experimental.pallas import tpu as pltpu

F32 = jnp.float32
BF16 = jnp.bfloat16

D_MODEL = 1024
PLE_DIM = 256
CHUNK = 128
EPS = 1e-6
M_HEADS = 4
M_QK = 128
M_V = 256
M_INNER = M_HEADS * M_V
S_INNER = 2048
S_HEADDIM = 64
S_HEADS = 32
S_GROUPS = 4
S_STATE = 128
S_CONV = 4
S_HEADS_PER_GROUP = S_HEADS // S_GROUPS
D_FF = 2816
LANES = 128
SUBLANES = 8

OFF_XS = 0
OFF_Z = 2048
OFF_GATE = 4096
OFF_V = 6144
OFF_O = 7168
OFF_Q = 8192
OFF_K = 8704
OFF_B = 9216
OFF_C = 9728
OFF_MISC = 10240
N_PROJ = OFF_MISC + LANES
MISC_I = 0
MISC_F = M_HEADS
MISC_DT = 2 * M_HEADS

VMEM_LIMIT = 56 * 1024 * 1024


def _dot(a, b):
    return jnp.dot(a, b, preferred_element_type=F32)


def _dot_nt(a, b):
    return lax.dot_general(a, b, (((1,), (1,)), ((), ())), preferred_element_type=F32)


def _rms(x, g):
    r = lax.rsqrt(jnp.mean(x * x, axis=-1, keepdims=True) + EPS)
    return x * r * g


def _sigmoid(x):
    return 1.0 / (1.0 + jnp.exp(-x))


def _silu(x):
    return x * _sigmoid(x)


def _split3(x):
    hi = x.astype(BF16)
    r = x - hi.astype(F32)
    mid = r.astype(BF16)
    lo = (r - mid.astype(F32)).astype(BF16)
    return hi, mid, lo


def _cumsum_rows(tril_bf16, x):
    hi, mid, lo = _split3(x)
    return _dot(tril_bf16, hi) + _dot(tril_bf16, mid) + _dot(tril_bf16, lo)


def _expand_lanes(x, e_ref):
    hi, mid, lo = _split3(x)
    return _dot(jnp.concatenate([hi, mid, lo], axis=1), e_ref[...])


def _ffn_kernel(*refs, n_chunks, has_ple, final_norm):
    h_ref, g_ref, w1_ref, w3_ref, w2_ref = refs[:5]
    pos = 5
    if has_ple:
        p_ref, gp_ref, wg_ref, wp_ref = refs[pos:pos + 4]
        pos += 4
    if final_norm:
        gf_ref = refs[pos]
        pos += 1
    o_ref = refs[pos]

    h = h_ref[...]
    u = _rms(h, g_ref[...]).astype(BF16)
    fc = D_FF // n_chunks
    acc = None
    for c in range(n_chunks):
        a = _dot(u, w1_ref[:, c * fc:(c + 1) * fc])
        b = _dot(u, w3_ref[:, c * fc:(c + 1) * fc])
        hid = (_silu(a) * b).astype(BF16)
        d = _dot(hid, w2_ref[c * fc:(c + 1) * fc, :])
        acc = d if acc is None else acc + d
    h = h + 0.5 * acc
    if has_ple:
        ug = _rms(h, gp_ref[...]).astype(BF16)
        gate = _sigmoid(_dot(ug, wg_ref[...]))
        pe = _dot(p_ref[...].astype(BF16), wp_ref[...])
        h = h + gate * pe
    if final_norm:
        h = _rms(h, gf_ref[...])
    o_ref[...] = h


def _const_spec(shape):
    nd = len(shape)
    return pl.BlockSpec(shape, lambda *_: (0,) * nd, pipeline_mode=pl.Buffered(1))


def _ffn(h, g, w1, w3, w2, ple=None, g_final=None, tm=512, n_chunks=2):
    t = h.shape[0]
    tm = min(tm, t)
    row = lambda i: (i, 0)
    in_specs = [pl.BlockSpec((tm, D_MODEL), row), _const_spec((1, D_MODEL)),
                _const_spec((D_MODEL, D_FF)), _const_spec((D_MODEL, D_FF)), _const_spec((D_FF, D_MODEL))]
    args = [h, g, w1, w3, w2]
    if ple is not None:
        p, gp, wg, wp = ple
        in_specs += [pl.BlockSpec((tm, PLE_DIM), row), _const_spec((1, D_MODEL)),
                     _const_spec((D_MODEL, D_MODEL)), _const_spec((PLE_DIM, D_MODEL))]
        args += [p, gp, wg, wp]
    if g_final is not None:
        in_specs.append(_const_spec((1, D_MODEL)))
        args.append(g_final)
    return pl.pallas_call(
        functools.partial(_ffn_kernel, n_chunks=n_chunks, has_ple=ple is not None,
                          final_norm=g_final is not None),
        grid=(t // tm,),
        in_specs=in_specs,
        out_specs=pl.BlockSpec((tm, D_MODEL), row),
        out_shape=jax.ShapeDtypeStruct((t, D_MODEL), F32),
        compiler_params=pltpu.CompilerParams(dimension_semantics=("parallel",),
                                             vmem_limit_bytes=VMEM_LIMIT),
        name="ffn",
    )(*args)


def _proj_kernel(h_ref, g_ref, w_ref, o_ref, u_ref):
    @pl.when(pl.program_id(1) == 0)
    def _():
        u_ref[...] = _rms(h_ref[...], g_ref[...]).astype(BF16)

    o_ref[...] = _dot(u_ref[...], w_ref[...])


def _proj(h, g, w, tm=1024, tn=1152):
    t = h.shape[0]
    tm = min(tm, t)
    return pl.pallas_call(
        _proj_kernel,
        grid=(t // tm, N_PROJ // tn),
        in_specs=[pl.BlockSpec((tm, D_MODEL), lambda i, j: (i, 0)),
                  pl.BlockSpec((1, D_MODEL), lambda i, j: (0, 0)),
                  pl.BlockSpec((D_MODEL, tn), lambda i, j: (0, j))],
        out_specs=pl.BlockSpec((tm, tn), lambda i, j: (i, j)),
        out_shape=jax.ShapeDtypeStruct((t, N_PROJ), F32),
        scratch_shapes=[pltpu.VMEM((tm, D_MODEL), BF16)],
        compiler_params=pltpu.CompilerParams(dimension_semantics=("parallel", "arbitrary"),
                                             vmem_limit_bytes=VMEM_LIMIT),
        name="proj",
    )(h, g, w)


def _mlstm_kernel(q_ref, k_ref, v_ref, o_ref, misc_ref, bias_ref, gh_ref, out_ref, c_st, n_st, m_st):
    L = CHUNK
    scale = M_QK ** -0.5

    @pl.when(pl.program_id(1) == 0)
    def _():
        c_st[...] = jnp.zeros_like(c_st)
        n_st[...] = jnp.zeros_like(n_st)
        m_st[...] = jnp.zeros_like(m_st)

    x = misc_ref[0] + bias_ref[...]
    lane = lax.broadcasted_iota(jnp.int32, (L, LANES), 1)
    logf = jnp.minimum(x, 0.0) - jnp.log1p(jnp.exp(-jnp.abs(x)))
    xg = jnp.where(lane < MISC_F, x, logf)
    t_idx = lax.broadcasted_iota(jnp.int32, (L, L), 0)
    s_idx = lax.broadcasted_iota(jnp.int32, (L, L), 1)
    causal = s_idx <= t_idx
    tril = jnp.where(causal, 1.0, 0.0).astype(BF16)
    bc = _cumsum_rows(tril, xg)
    xg_t = xg.T
    bc_t = bc.T

    for h in range(M_HEADS):
        q = q_ref[0, :, h * M_QK:(h + 1) * M_QK]
        k = k_ref[0, :, h * M_QK:(h + 1) * M_QK]
        v = v_ref[0, :, h * M_V:(h + 1) * M_V]
        li_col = xg[:, MISC_I + h:MISC_I + h + 1]
        bc_col = bc[:, MISC_F + h:MISC_F + h + 1]
        li_row = xg_t[MISC_I + h:MISC_I + h + 1, :]
        bc_row = bc_t[MISC_F + h:MISC_F + h + 1, :]
        gtot = bc_col[L - 1:L, :]
        m_prev = m_st[h:h + 1, 0:1]
        n_prev = n_st[h:h + 1, :]
        c_prev = c_st[h]

        dlog = jnp.where(causal, bc_col - bc_row + li_row, -jnp.inf)
        m_inter = bc_col + m_prev
        m_t = jnp.maximum(jnp.max(dlog, axis=1, keepdims=True), m_inter)
        w_intra = jnp.exp(dlog - m_t)
        w_inter = jnp.exp(m_inter - m_t)

        qb = q.astype(BF16)
        kb = k.astype(BF16)
        vb = v.astype(BF16)
        sc = _dot_nt(qb, kb) * scale * w_intra
        qn = jnp.sum(q * n_prev, axis=1, keepdims=True) * scale
        den = jnp.sum(sc, axis=1, keepdims=True) + w_inter * qn
        num = _dot(sc.astype(BF16), vb) + _dot(qb, c_prev.astype(BF16)) * (scale * w_inter)
        hh = num / jnp.maximum(jnp.abs(den), jnp.exp(-m_t))

        r = lax.rsqrt(jnp.mean(hh * hh, axis=1, keepdims=True) + EPS)
        og = _sigmoid(o_ref[0, :, h * M_V:(h + 1) * M_V])
        out_ref[0, :, h * M_V:(h + 1) * M_V] = (hh * r * gh_ref[h:h + 1, :] * og).astype(out_ref.dtype)

        a_col = gtot - bc_col + li_col
        m_loc = jnp.max(a_col, axis=0, keepdims=True)
        kw = k * jnp.exp(a_col - m_loc)
        c_loc = _dot(kw.T.astype(BF16), vb)
        n_loc = jnp.sum(kw, axis=0, keepdims=True)
        m_new = jnp.maximum(gtot + m_prev, m_loc)
        sp = jnp.exp(gtot + m_prev - m_new)
        sl = jnp.exp(m_loc - m_new)
        c_st[h] = sp * c_prev + sl * c_loc
        n_st[h:h + 1, :] = sp * n_prev + sl * n_loc
        m_st[h:h + 1, :] = jnp.broadcast_to(m_new, (1, LANES))


def _mlstm(proj3, gate_bias, g_head):
    bsz, s_len, _ = proj3.shape
    L = CHUNK
    blk = lambda width, off: pl.BlockSpec((1, L, width), lambda b, c: (b, c, off // width))
    return pl.pallas_call(
        _mlstm_kernel,
        grid=(bsz, s_len // L),
        in_specs=[blk(M_HEADS * M_QK, OFF_Q), blk(M_HEADS * M_QK, OFF_K), blk(M_INNER, OFF_V),
                  blk(M_INNER, OFF_O), blk(LANES, OFF_MISC),
                  pl.BlockSpec((1, LANES), lambda b, c: (0, 0)),
                  pl.BlockSpec((M_HEADS, M_V), lambda b, c: (0, 0))],
        out_specs=pl.BlockSpec((1, L, M_INNER), lambda b, c: (b, c, 0)),
        out_shape=jax.ShapeDtypeStruct((bsz, s_len, M_INNER), BF16),
        scratch_shapes=[pltpu.VMEM((M_HEADS, M_QK, M_V), F32),
                        pltpu.VMEM((SUBLANES, M_QK), F32),
                        pltpu.VMEM((SUBLANES, LANES), F32)],
        compiler_params=pltpu.CompilerParams(dimension_semantics=("parallel", "arbitrary"),
                                             vmem_limit_bytes=VMEM_LIMIT),
        name="mlstm",
    )(proj3, proj3, proj3, proj3, proj3, gate_bias, g_head)


def _conv_silu(cur_ref, pad_ref, w_ref, b_ref, first):
    L = CHUNK

    @pl.when(first)
    def _():
        pad_ref[0:SUBLANES, :] = jnp.zeros((SUBLANES, pad_ref.shape[1]), F32)

    pad_ref[SUBLANES:SUBLANES + L, :] = cur_ref[0]
    acc = b_ref[...] + w_ref[S_CONV - 1:S_CONV, :] * pad_ref[SUBLANES:SUBLANES + L, :]
    for j in range(S_CONV - 1):
        start = SUBLANES - (S_CONV - 1) + j
        acc = acc + w_ref[j:j + 1, :] * pad_ref[start:start + L, :]
    pad_ref[0:SUBLANES, :] = pad_ref[L:L + SUBLANES, :]
    return _silu(acc)


def _ssd_kernel(xs_ref, z_ref, b_ref, c_ref, misc_ref, cwx_ref, cwb_ref, cwc_ref, cbx_ref, cbb_ref, cbc_ref,
                dtb_ref, alog_ref, dsk_ref, g_ref, e128_ref, e64_ref, out_ref,
                padx, padb, padc, st_ref, acscb_ref, y_ref):
    L = CHUNK
    first = pl.program_id(1) == 0

    @pl.when(first)
    def _():
        st_ref[...] = jnp.zeros_like(st_ref)

    xs = _conv_silu(xs_ref, padx, cwx_ref, cbx_ref, first)
    bm = _conv_silu(b_ref, padb, cwb_ref, cbb_ref, first)
    cm = _conv_silu(c_ref, padc, cwc_ref, cbc_ref, first)

    lane = lax.broadcasted_iota(jnp.int32, (L, LANES), 1)
    dt_lanes = (lane >= MISC_DT) & (lane < MISC_DT + S_HEADS)
    xm = misc_ref[0] + dtb_ref[...]
    dt = jnp.maximum(xm, 0.0) + jnp.log1p(jnp.exp(-jnp.abs(xm)))
    dt = jnp.where(dt_lanes, dt, 0.0)
    a_neg = jnp.where(dt_lanes[0:1, :], -jnp.exp(alog_ref[...]), 0.0)
    t_idx = lax.broadcasted_iota(jnp.int32, (L, L), 0)
    s_idx = lax.broadcasted_iota(jnp.int32, (L, L), 1)
    causal = s_idx <= t_idx
    tril = jnp.where(causal, 1.0, 0.0).astype(BF16)
    acs = _cumsum_rows(tril, dt * a_neg)
    acs_t = acs.T
    dt_t = dt.T
    acscb_ref[...] = _expand_lanes(acs, e128_ref)
    acs64 = _expand_lanes(acs, e64_ref)
    dt64 = _expand_lanes(dt, e64_ref)
    last64 = acs64[L - 1:L, :]
    dtw64 = dt64 * jnp.exp(last64 - acs64)
    gc64 = jnp.exp(last64)
    lane_lo = lane < S_HEADDIM

    for g in range(S_GROUPS):
        bg = bm[:, g * S_STATE:(g + 1) * S_STATE]
        cg = cm[:, g * S_STATE:(g + 1) * S_STATE]
        cb = _dot_nt(cg.astype(BF16), bg.astype(BF16))
        for e in range(0, S_HEADS_PER_GROUP, 2):
            h0 = g * S_HEADS_PER_GROUP + e
            j = h0 // 2
            x_pair = xs[:, j * LANES:(j + 1) * LANES]
            st_pair = st_ref[:, j * LANES:(j + 1) * LANES]
            rhs = jnp.concatenate([x_pair, st_pair], axis=0).astype(BF16)
            ys = []
            for h in (h0, h0 + 1):
                col = acscb_ref[:, h * LANES:(h + 1) * LANES]
                row = acs_t[MISC_DT + h:MISC_DT + h + 1, :]
                dt_row = dt_t[MISC_DT + h:MISC_DT + h + 1, :]
                w = cb * jnp.exp(jnp.where(causal, col - row, -jnp.inf)) * dt_row
                lhs = jnp.concatenate([w, cg * jnp.exp(col)], axis=1).astype(BF16)
                ys.append(_dot(lhs, rhs))
            y_pair = jnp.where(lane_lo, ys[0], ys[1])
            y_ref[:, j * LANES:(j + 1) * LANES] = y_pair + x_pair * dsk_ref[:, j * LANES:(j + 1) * LANES]
        gw = S_HEADS_PER_GROUP * S_HEADDIM
        xdtw = (xs[:, g * gw:(g + 1) * gw] * dtw64[:, g * gw:(g + 1) * gw]).astype(BF16)
        st_ref[:, g * gw:(g + 1) * gw] = (st_ref[:, g * gw:(g + 1) * gw] * gc64[:, g * gw:(g + 1) * gw]
                                          + _dot(bg.T.astype(BF16), xdtw))

    y = y_ref[...] * _silu(z_ref[0])
    out_ref[0] = _rms(y, g_ref[...]).astype(out_ref.dtype)


def _expansion_matrix(width):
    e = np.zeros((LANES, S_HEADS * width), np.float32)
    for h in range(S_HEADS):
        e[MISC_DT + h, h * width:(h + 1) * width] = 1.0
    return jnp.asarray(np.concatenate([e, e, e], axis=0), dtype=BF16)


def _ssd(proj3, cw, cb, dtb, alog, dsk, g_out):
    bsz, s_len, _ = proj3.shape
    L = CHUNK
    blk = lambda width, off: pl.BlockSpec((1, L, width), lambda b, c: (b, c, off // width))
    gn = S_GROUPS * S_STATE
    cwx, cwb, cwc = cw[:, :S_INNER], cw[:, S_INNER:S_INNER + gn], cw[:, S_INNER + gn:]
    cbx, cbb, cbc = cb[:, :S_INNER], cb[:, S_INNER:S_INNER + gn], cb[:, S_INNER + gn:]
    e128 = _expansion_matrix(LANES)
    e64 = _expansion_matrix(S_HEADDIM)
    consts = [cwx, cwb, cwc, cbx, cbb, cbc, dtb, alog, dsk, g_out, e128, e64]
    return pl.pallas_call(
        _ssd_kernel,
        grid=(bsz, s_len // L),
        in_specs=[blk(S_INNER, OFF_XS), blk(S_INNER, OFF_Z), blk(gn, OFF_B), blk(gn, OFF_C),
                  blk(LANES, OFF_MISC)] + [_const_spec(a.shape) for a in consts],
        out_specs=pl.BlockSpec((1, L, S_INNER), lambda b, c: (b, c, 0)),
        out_shape=jax.ShapeDtypeStruct((bsz, s_len, S_INNER), BF16),
        scratch_shapes=[pltpu.VMEM((L + SUBLANES, S_INNER), F32),
                        pltpu.VMEM((L + SUBLANES, gn), F32),
                        pltpu.VMEM((L + SUBLANES, gn), F32),
                        pltpu.VMEM((S_STATE, S_INNER), F32),
                        pltpu.VMEM((L, S_HEADS * LANES), F32),
                        pltpu.VMEM((L, S_INNER), F32)],
        compiler_params=pltpu.CompilerParams(dimension_semantics=("parallel", "arbitrary"),
                                             vmem_limit_bytes=VMEM_LIMIT),
        name="ssd",
    )(proj3, proj3, proj3, proj3, proj3, *consts)


def _merge_kernel(h_ref, ha_ref, yb_ref, gate_ref, wa_ref, wb_ref, wo_ref, o_ref):
    gate = _sigmoid(gate_ref[...])
    a = _dot(ha_ref[...], wa_ref[...])
    b = _dot(yb_ref[...], wb_ref[...])
    merged = gate[:, :D_MODEL] * a + gate[:, D_MODEL:] * b
    o_ref[...] = h_ref[...] + _dot(merged.astype(BF16), wo_ref[...])


def _merge(h, ha, yb, proj, wa, wb, wo, tm=512):
    t = h.shape[0]
    tm = min(tm, t)
    row = lambda i: (i, 0)
    return pl.pallas_call(
        _merge_kernel,
        grid=(t // tm,),
        in_specs=[pl.BlockSpec((tm, D_MODEL), row), pl.BlockSpec((tm, M_INNER), row),
                  pl.BlockSpec((tm, S_INNER), row),
                  pl.BlockSpec((tm, 2 * D_MODEL), lambda i: (i, OFF_GATE // (2 * D_MODEL))),
                  _const_spec((M_INNER, D_MODEL)), _const_spec((S_INNER, D_MODEL)),
                  _const_spec((D_MODEL, D_MODEL))],
        out_specs=pl.BlockSpec((tm, D_MODEL), row),
        out_shape=jax.ShapeDtypeStruct((t, D_MODEL), F32),
        compiler_params=pltpu.CompilerParams(dimension_semantics=("parallel",),
                                             vmem_limit_bytes=VMEM_LIMIT),
        name="merge",
    )(h, ha, yb, proj, wa, wb, wo)


def _reorder_w_in(w_in):
    sizes = (M_HEADS * M_QK, M_HEADS * M_QK, M_INNER, M_INNER, M_HEADS, M_HEADS, S_INNER,
             S_INNER, S_GROUPS * S_STATE, S_GROUPS * S_STATE, S_HEADS, 2 * D_MODEL)
    cuts = [int(c) for c in np.cumsum(sizes)[:-1]]
    q, k, v, o, ig, fg, z, xs, bm, cm, dt, gate = jnp.split(w_in, cuts, axis=-1)
    pad = jnp.zeros(w_in.shape[:-1] + (LANES - MISC_DT - S_HEADS,), w_in.dtype)
    return jnp.concatenate([xs, z, gate, v, o, q, k, bm, cm, ig, fg, dt, pad], axis=-1).astype(BF16)


def _misc_row(depth, pieces):
    row = jnp.zeros((depth, LANES), F32)
    for off, val in pieces:
        row = row.at[:, off:off + val.shape[-1]].set(val.astype(F32))
    return row[:, None, :]


def kernel(x, p, g_ffn1, w1_ffn1, w3_ffn1, w2_ffn1, g_mix, w_in, b_igate, b_fgate, g_mlstm_head, conv_w, conv_b, dt_bias, a_log, d_skip, g_ssm_out, w_branch_a, w_branch_b, w_out, g_ffn2, w1_ffn2, w3_ffn2, w2_ffn2, g_ple, w_ple_gate, w_ple, g_final):
    bsz, s_len, _ = x.shape
    depth = w_in.shape[0]
    t = bsz * s_len
    bf = lambda a: a.astype(BF16)
    w1a, w3a, w2a = bf(w1_ffn1), bf(w3_ffn1), bf(w2_ffn1)
    w1b, w3b, w2b = bf(w1_ffn2), bf(w3_ffn2), bf(w2_ffn2)
    w_proj = _reorder_w_in(w_in)
    wa, wb, wo = bf(w_branch_a), bf(w_branch_b), bf(w_out)
    wg, wp = bf(w_ple_gate), bf(w_ple)
    gate_bias = _misc_row(depth, [(MISC_I, b_igate), (MISC_F, b_fgate)])
    dtb = _misc_row(depth, [(MISC_DT, dt_bias)])
    alog = _misc_row(depth, [(MISC_DT, a_log)])
    dsk = jnp.repeat(d_skip, S_HEADDIM, axis=-1)[:, None, :]
    row = lambda a, i: a[i][None, :]

    h = x.reshape(t, D_MODEL)
    p2 = p.reshape(depth, t, PLE_DIM)
    for i in range(depth):
        h = _ffn(h, row(g_ffn1, i), w1a[i], w3a[i], w2a[i])
        proj = _proj(h, row(g_mix, i), w_proj[i])
        proj3 = proj.reshape(bsz, s_len, N_PROJ)
        ha = _mlstm(proj3, gate_bias[i], g_mlstm_head[i])
        yb = _ssd(proj3, conv_w[i], row(conv_b, i), dtb[i], alog[i], dsk[i], row(g_ssm_out, i))
        h = _merge(h, ha.reshape(t, M_INNER), yb.reshape(t, S_INNER), proj, wa[i], wb[i], wo[i])
        h = _ffn(h, row(g_ffn2, i), w1b[i], w3b[i], w2b[i],
                 ple=(p2[i], row(g_ple, i), wg[i], wp[i]),
                 g_final=g_final[None, :] if i == depth - 1 else None)
    return h.reshape(bsz, s_len, D_MODEL)
```

```python
import functools

import numpy as np
import jax
import jax.numpy as jnp
from jax import lax
from jax.experimental import pallas as pl
from jax.experimental.pallas import tpu as pltpu

F32 = jnp.float32
BF16 = jnp.bfloat16

D_MODEL = 1024
PLE_DIM = 256
CHUNK = 128
EPS = 1e-6
M_HEADS = 4
M_QK = 128
M_V = 256
M_INNER = M_HEADS * M_V
S_INNER = 2048
S_HEADDIM = 64
S_HEADS = 32
S_GROUPS = 4
S_STATE = 128
S_CONV = 4
S_HEADS_PER_GROUP = S_HEADS // S_GROUPS
D_FF = 2816
LANES = 128
SUBLANES = 8
MXU_DIM = 256

OFF_XS = 0
OFF_Z = 2048
OFF_GATE = 4096
OFF_V = 6144
OFF_O = 7168
OFF_Q = 8192
OFF_K = 8704
OFF_B = 9216
OFF_C = 9728
N_PROJ = 10240
MISC_I = 0
MISC_F = M_HEADS
MISC_DT = 2 * M_HEADS

VMEM_LIMIT = 56 * 1024 * 1024
FFN_CHUNKS = (4 * MXU_DIM, 4 * MXU_DIM, D_FF - 8 * MXU_DIM)


def _dot(a, b):
    return jnp.dot(a, b, preferred_element_type=F32)


def _dot_nt(a, b):
    return lax.dot_general(a, b, (((1,), (1,)), ((), ())), preferred_element_type=F32)


def _rms(x, g):
    r = lax.rsqrt(jnp.mean(x * x, axis=-1, keepdims=True) + EPS)
    return x * r * g


def _sigmoid(x):
    return 1.0 / (1.0 + jnp.exp(-x))


def _silu(x):
    return x * _sigmoid(x)


def _split3(x):
    hi = x.astype(BF16)
    r = x - hi.astype(F32)
    mid = r.astype(BF16)
    lo = (r - mid.astype(F32)).astype(BF16)
    return hi, mid, lo


def _cumsum_rows(tril_bf16, x):
    hi, mid, lo = _split3(x)
    return _dot(tril_bf16, hi) + _dot(tril_bf16, mid) + _dot(tril_bf16, lo)


def _expand_lanes(x, e_ref):
    hi, mid, lo = _split3(x)
    return _dot(jnp.concatenate([hi, mid, lo], axis=1), e_ref[...])


def _layer_spec(shape, layer):
    nd = len(shape)
    return pl.BlockSpec((None,) + tuple(shape), lambda *_: (layer,) + (0,) * nd,
                        pipeline_mode=pl.Buffered(1))


def _const_spec(shape):
    nd = len(shape)
    return pl.BlockSpec(shape, lambda *_: (0,) * nd, pipeline_mode=pl.Buffered(1))


def _ffn_kernel(*refs, has_ple, final_norm):
    h_ref, g_ref, w1_ref, w3_ref, w2_ref = refs[:5]
    pos = 5
    if has_ple:
        p_ref, gp_ref, wg_ref, wp_ref = refs[pos:pos + 4]
        pos += 4
    if final_norm:
        gf_ref = refs[pos]
        pos += 1
    o_ref = refs[pos]

    h = h_ref[...]
    u = _rms(h, g_ref[...]).astype(BF16)
    acc = None
    lo = 0
    for fc in FFN_CHUNKS:
        a = _dot(u, w1_ref[:, lo:lo + fc])
        b = _dot(u, w3_ref[:, lo:lo + fc])
        hid = (_silu(a) * b).astype(BF16)
        d = _dot(hid, w2_ref[lo:lo + fc, :])
        acc = d if acc is None else acc + d
        lo += fc
    h = h + 0.5 * acc
    if has_ple:
        ug = _rms(h, gp_ref[...]).astype(BF16)
        gate = _sigmoid(_dot(ug, wg_ref[...]))
        pe = _dot(p_ref[...].astype(BF16), wp_ref[...])
        h = h + gate * pe
    if final_norm:
        h = _rms(h, gf_ref[...])
    o_ref[...] = h


def _ffn(h, layer, g, w1, w3, w2, ple=None, g_final=None, tm=512):
    t = h.shape[0]
    tm = min(tm, t)
    row = lambda i: (i, 0)
    in_specs = [pl.BlockSpec((tm, D_MODEL), row), _layer_spec((1, D_MODEL), layer),
                _layer_spec((D_MODEL, D_FF), layer), _layer_spec((D_MODEL, D_FF), layer),
                _layer_spec((D_FF, D_MODEL), layer)]
    args = [h, g, w1, w3, w2]
    if ple is not None:
        p, gp, wg, wp = ple
        in_specs += [pl.BlockSpec((None, tm, PLE_DIM), lambda i: (layer, i, 0)),
                     _layer_spec((1, D_MODEL), layer), _layer_spec((D_MODEL, D_MODEL), layer),
                     _layer_spec((PLE_DIM, D_MODEL), layer)]
        args += [p, gp, wg, wp]
    if g_final is not None:
        in_specs.append(_const_spec((1, D_MODEL)))
        args.append(g_final)
    return pl.pallas_call(
        functools.partial(_ffn_kernel, has_ple=ple is not None, final_norm=g_final is not None),
        grid=(t // tm,),
        in_specs=in_specs,
        out_specs=pl.BlockSpec((tm, D_MODEL), row),
        out_shape=jax.ShapeDtypeStruct((t, D_MODEL), F32),
        compiler_params=pltpu.CompilerParams(dimension_semantics=("parallel",),
                                             vmem_limit_bytes=VMEM_LIMIT),
        name="ffn",
    )(*args)


def _proj_kernel(h_ref, g_ref, w_ref, wm_ref, o_ref, misc_ref, u_ref):
    @pl.when(pl.program_id(1) == 0)
    def _():
        u = _rms(h_ref[...], g_ref[...]).astype(BF16)
        u_ref[...] = u
        misc_ref[...] = _dot(u, wm_ref[...])

    o_ref[...] = _dot(u_ref[...], w_ref[...]).astype(o_ref.dtype)


def _proj(h, layer, g, w, w_misc, tm=1024, tn=1024):
    t = h.shape[0]
    tm = min(tm, t)
    return pl.pallas_call(
        _proj_kernel,
        grid=(t // tm, N_PROJ // tn),
        in_specs=[pl.BlockSpec((tm, D_MODEL), lambda i, j: (i, 0)),
                  pl.BlockSpec((None, 1, D_MODEL), lambda i, j: (layer, 0, 0)),
                  pl.BlockSpec((None, D_MODEL, tn), lambda i, j: (layer, 0, j)),
                  pl.BlockSpec((None, D_MODEL, LANES), lambda i, j: (layer, 0, 0))],
        out_specs=[pl.BlockSpec((tm, tn), lambda i, j: (i, j)),
                   pl.BlockSpec((tm, LANES), lambda i, j: (i, 0))],
        out_shape=[jax.ShapeDtypeStruct((t, N_PROJ), BF16), jax.ShapeDtypeStruct((t, LANES), F32)],
        scratch_shapes=[pltpu.VMEM((tm, D_MODEL), BF16)],
        compiler_params=pltpu.CompilerParams(dimension_semantics=("parallel", "arbitrary"),
                                             vmem_limit_bytes=VMEM_LIMIT),
        name="proj",
    )(h, g, w, w_misc)


def _mlstm_kernel(q_ref, k_ref, v_ref, o_ref, misc_ref, bias_ref, gh_ref, out_ref, c_st, n_st, m_st):
    L = CHUNK
    scale = M_QK ** -0.5

    @pl.when(pl.program_id(1) == 0)
    def _():
        c_st[...] = jnp.zeros_like(c_st)
        n_st[...] = jnp.zeros_like(n_st)
        m_st[...] = jnp.zeros_like(m_st)

    x = misc_ref[0] + bias_ref[...]
    lane = lax.broadcasted_iota(jnp.int32, (L, LANES), 1)
    logf = jnp.minimum(x, 0.0) - jnp.log1p(jnp.exp(-jnp.abs(x)))
    xg = jnp.where(lane < MISC_F, x, logf)
    t_idx = lax.broadcasted_iota(jnp.int32, (L, L), 0)
    s_idx = lax.broadcasted_iota(jnp.int32, (L, L), 1)
    causal = s_idx <= t_idx
    tril = jnp.where(causal, 1.0, 0.0).astype(BF16)
    bc = _cumsum_rows(tril, xg)
    xg_t = xg.T
    bc_t = bc.T

    for h in range(M_HEADS):
        qb = q_ref[0, :, h * M_QK:(h + 1) * M_QK]
        kb = k_ref[0, :, h * M_QK:(h + 1) * M_QK]
        vb = v_ref[0, :, h * M_V:(h + 1) * M_V]
        q = qb.astype(F32)
        k = kb.astype(F32)
        li_col = xg[:, MISC_I + h:MISC_I + h + 1]
        bc_col = bc[:, MISC_F + h:MISC_F + h + 1]
        li_row = xg_t[MISC_I + h:MISC_I + h + 1, :]
        bc_row = bc_t[MISC_F + h:MISC_F + h + 1, :]
        gtot = bc_col[L - 1:L, :]
        m_prev = m_st[h:h + 1, 0:1]
        n_prev = n_st[h:h + 1, :]
        c_prev = c_st[h]

        dlog = jnp.where(causal, bc_col - bc_row + li_row, -jnp.inf)
        m_inter = bc_col + m_prev
        m_t = jnp.maximum(jnp.max(dlog, axis=1, keepdims=True), m_inter)
        w_intra = jnp.exp(dlog - m_t)
        w_inter = jnp.exp(m_inter - m_t)

        sc = _dot_nt(qb, kb) * scale * w_intra
        qn = jnp.sum(q * n_prev, axis=1, keepdims=True) * scale
        den = jnp.sum(sc, axis=1, keepdims=True) + w_inter * qn
        num = _dot(sc.astype(BF16), vb) + _dot(qb, c_prev.astype(BF16)) * (scale * w_inter)
        hh = num / jnp.maximum(jnp.abs(den), jnp.exp(-m_t))

        r = lax.rsqrt(jnp.mean(hh * hh, axis=1, keepdims=True) + EPS)
        og = _sigmoid(o_ref[0, :, h * M_V:(h + 1) * M_V].astype(F32))
        out_ref[0, :, h * M_V:(h + 1) * M_V] = (hh * r * gh_ref[h:h + 1, :] * og).astype(out_ref.dtype)

        a_col = gtot - bc_col + li_col
        m_loc = jnp.max(a_col, axis=0, keepdims=True)
        kw = k * jnp.exp(a_col - m_loc)
        c_loc = _dot(kw.T.astype(BF16), vb)
        n_loc = jnp.sum(kw, axis=0, keepdims=True)
        m_new = jnp.maximum(gtot + m_prev, m_loc)
        sp = jnp.exp(gtot + m_prev - m_new)
        sl = jnp.exp(m_loc - m_new)
        c_st[h] = sp * c_prev + sl * c_loc
        n_st[h:h + 1, :] = sp * n_prev + sl * n_loc
        m_st[h:h + 1, :] = jnp.broadcast_to(m_new, (1, LANES))


def _chunk_spec(width, off):
    return pl.BlockSpec((1, CHUNK, width), lambda b, c: (b, c, off // width))


def _mlstm(proj3, misc3, layer, gate_bias, g_head):
    bsz, s_len, _ = proj3.shape
    return pl.pallas_call(
        _mlstm_kernel,
        grid=(bsz, s_len // CHUNK),
        in_specs=[_chunk_spec(M_HEADS * M_QK, OFF_Q), _chunk_spec(M_HEADS * M_QK, OFF_K),
                  _chunk_spec(M_INNER, OFF_V), _chunk_spec(M_INNER, OFF_O), _chunk_spec(LANES, 0),
                  _layer_spec((1, LANES), layer), _layer_spec((M_HEADS, M_V), layer)],
        out_specs=pl.BlockSpec((1, CHUNK, M_INNER), lambda b, c: (b, c, 0)),
        out_shape=jax.ShapeDtypeStruct((bsz, s_len, M_INNER), BF16),
        scratch_shapes=[pltpu.VMEM((M_HEADS, M_QK, M_V), F32),
                        pltpu.VMEM((SUBLANES, M_QK), F32),
                        pltpu.VMEM((SUBLANES, LANES), F32)],
        compiler_params=pltpu.CompilerParams(dimension_semantics=("parallel", "arbitrary"),
                                             vmem_limit_bytes=VMEM_LIMIT),
        name="mlstm",
    )(proj3, proj3, proj3, proj3, misc3, gate_bias, g_head)


def _conv_silu(cur_ref, pad_ref, w_ref, b_ref, first):
    L = CHUNK

    @pl.when(first)
    def _():
        pad_ref[0:SUBLANES, :] = jnp.zeros((SUBLANES, pad_ref.shape[1]), F32)

    pad_ref[SUBLANES:SUBLANES + L, :] = cur_ref[0].astype(F32)
    acc = b_ref[...] + w_ref[S_CONV - 1:S_CONV, :] * pad_ref[SUBLANES:SUBLANES + L, :]
    for j in range(S_CONV - 1):
        start = SUBLANES - (S_CONV - 1) + j
        acc = acc + w_ref[j:j + 1, :] * pad_ref[start:start + L, :]
    pad_ref[0:SUBLANES, :] = pad_ref[L:L + SUBLANES, :]
    return _silu(acc)


def _ssd_kernel(xs_ref, z_ref, b_ref, c_ref, misc_ref, cwx_ref, cwb_ref, cwc_ref, cbx_ref, cbb_ref, cbc_ref,
                dtb_ref, alog_ref, dsk_ref, g_ref, e128_ref, e64_ref, out_ref,
                padx, padb, padc, st_ref, acscb_ref, y_ref):
    L = CHUNK
    first = pl.program_id(1) == 0

    @pl.when(first)
    def _():
        st_ref[...] = jnp.zeros_like(st_ref)

    xs = _conv_silu(xs_ref, padx, cwx_ref, cbx_ref, first)
    bm = _conv_silu(b_ref, padb, cwb_ref, cbb_ref, first)
    cm = _conv_silu(c_ref, padc, cwc_ref, cbc_ref, first)

    lane = lax.broadcasted_iota(jnp.int32, (L, LANES), 1)
    dt_lanes = (lane >= MISC_DT) & (lane < MISC_DT + S_HEADS)
    xm = misc_ref[0] + dtb_ref[...]
    dt = jnp.maximum(xm, 0.0) + jnp.log1p(jnp.exp(-jnp.abs(xm)))
    dt = jnp.where(dt_lanes, dt, 0.0)
    a_neg = jnp.where(dt_lanes[0:1, :], -jnp.exp(alog_ref[...]), 0.0)
    t_idx = lax.broadcasted_iota(jnp.int32, (L, L), 0)
    s_idx = lax.broadcasted_iota(jnp.int32, (L, L), 1)
    causal = s_idx <= t_idx
    tril = jnp.where(causal, 1.0, 0.0).astype(BF16)
    acs = _cumsum_rows(tril, dt * a_neg)
    acs_t = acs.T
    dt_t = dt.T
    acscb_ref[...] = _expand_lanes(acs, e128_ref)
    acs64 = _expand_lanes(acs, e64_ref)
    dt64 = _expand_lanes(dt, e64_ref)
    last64 = acs64[L - 1:L, :]
    dtw64 = dt64 * jnp.exp(last64 - acs64)
    gc64 = jnp.exp(last64)
    lane_lo = lane < S_HEADDIM

    for g in range(S_GROUPS):
        bg = bm[:, g * S_STATE:(g + 1) * S_STATE]
        cg = cm[:, g * S_STATE:(g + 1) * S_STATE]
        cb = _dot_nt(cg.astype(BF16), bg.astype(BF16))
        for e in range(0, S_HEADS_PER_GROUP, 2):
            h0 = g * S_HEADS_PER_GROUP + e
            j = h0 // 2
            x_pair = xs[:, j * LANES:(j + 1) * LANES]
            st_pair = st_ref[:, j * LANES:(j + 1) * LANES]
            rhs = jnp.concatenate([x_pair, st_pair], axis=0).astype(BF16)
            ys = []
            for h in (h0, h0 + 1):
                col = acscb_ref[:, h * LANES:(h + 1) * LANES]
                row = acs_t[MISC_DT + h:MISC_DT + h + 1, :]
                dt_row = dt_t[MISC_DT + h:MISC_DT + h + 1, :]
                w = cb * jnp.exp(jnp.where(causal, col - row, -jnp.inf)) * dt_row
                lhs = jnp.concatenate([w, cg * jnp.exp(col)], axis=1).astype(BF16)
                ys.append(_dot(lhs, rhs))
            y_pair = jnp.where(lane_lo, ys[0], ys[1])
            y_ref[:, j * LANES:(j + 1) * LANES] = y_pair + x_pair * dsk_ref[:, j * LANES:(j + 1) * LANES]
        gw = S_HEADS_PER_GROUP * S_HEADDIM
        xdtw = (xs[:, g * gw:(g + 1) * gw] * dtw64[:, g * gw:(g + 1) * gw]).astype(BF16)
        st_ref[:, g * gw:(g + 1) * gw] = (st_ref[:, g * gw:(g + 1) * gw] * gc64[:, g * gw:(g + 1) * gw]
                                          + _dot(bg.T.astype(BF16), xdtw))

    y = y_ref[...] * _silu(z_ref[0].astype(F32))
    out_ref[0] = _rms(y, g_ref[...]).astype(out_ref.dtype)


def _expansion_matrix(width):
    e = np.zeros((LANES, S_HEADS * width), np.float32)
    for h in range(S_HEADS):
        e[MISC_DT + h, h * width:(h + 1) * width] = 1.0
    return jnp.asarray(np.concatenate([e, e, e], axis=0), dtype=BF16)


def _ssd(proj3, misc3, layer, cwx, cwb, cwc, cbx, cbb, cbc, dtb, alog, dsk, g_out):
    bsz, s_len, _ = proj3.shape
    L = CHUNK
    gn = S_GROUPS * S_STATE
    e128 = _expansion_matrix(LANES)
    e64 = _expansion_matrix(S_HEADDIM)
    stacked = [cwx, cwb, cwc, cbx, cbb, cbc, dtb, alog, dsk, g_out]
    return pl.pallas_call(
        _ssd_kernel,
        grid=(bsz, s_len // L),
        in_specs=[_chunk_spec(S_INNER, OFF_XS), _chunk_spec(S_INNER, OFF_Z), _chunk_spec(gn, OFF_B),
                  _chunk_spec(gn, OFF_C), _chunk_spec(LANES, 0)]
                 + [_layer_spec(a.shape[1:], layer) for a in stacked]
                 + [_const_spec(e128.shape), _const_spec(e64.shape)],
        out_specs=pl.BlockSpec((1, L, S_INNER), lambda b, c: (b, c, 0)),
        out_shape=jax.ShapeDtypeStruct((bsz, s_len, S_INNER), BF16),
        scratch_shapes=[pltpu.VMEM((L + SUBLANES, S_INNER), F32),
                        pltpu.VMEM((L + SUBLANES, gn), F32),
                        pltpu.VMEM((L + SUBLANES, gn), F32),
                        pltpu.VMEM((S_STATE, S_INNER), F32),
                        pltpu.VMEM((L, S_HEADS * LANES), F32),
                        pltpu.VMEM((L, S_INNER), F32)],
        compiler_params=pltpu.CompilerParams(dimension_semantics=("parallel", "arbitrary"),
                                             vmem_limit_bytes=VMEM_LIMIT),
        name="ssd",
    )(proj3, proj3, proj3, proj3, misc3, *stacked, e128, e64)


def _merge_kernel(h_ref, ha_ref, yb_ref, gate_ref, wa_ref, wb_ref, wo_ref, o_ref):
    gate = _sigmoid(gate_ref[...].astype(F32))
    a = _dot(ha_ref[...], wa_ref[...])
    b = _dot(yb_ref[...], wb_ref[...])
    merged = gate[:, :D_MODEL] * a + gate[:, D_MODEL:] * b
    o_ref[...] = h_ref[...] + _dot(merged.astype(BF16), wo_ref[...])


def _merge(h, ha, yb, proj, layer, wa, wb, wo, tm=512):
    t = h.shape[0]
    tm = min(tm, t)
    row = lambda i: (i, 0)
    return pl.pallas_call(
        _merge_kernel,
        grid=(t // tm,),
        in_specs=[pl.BlockSpec((tm, D_MODEL), row), pl.BlockSpec((tm, M_INNER), row),
                  pl.BlockSpec((tm, S_INNER), row),
                  pl.BlockSpec((tm, 2 * D_MODEL), lambda i: (i, OFF_GATE // (2 * D_MODEL))),
                  _layer_spec((M_INNER, D_MODEL), layer), _layer_spec((S_INNER, D_MODEL), layer),
                  _layer_spec((D_MODEL, D_MODEL), layer)],
        out_specs=pl.BlockSpec((tm, D_MODEL), row),
        out_shape=jax.ShapeDtypeStruct((t, D_MODEL), F32),
        compiler_params=pltpu.CompilerParams(dimension_semantics=("parallel",),
                                             vmem_limit_bytes=VMEM_LIMIT),
        name="merge",
    )(h, ha, yb, proj, wa, wb, wo)


def _reorder_w_in(w_in):
    sizes = (M_HEADS * M_QK, M_HEADS * M_QK, M_INNER, M_INNER, M_HEADS, M_HEADS, S_INNER,
             S_INNER, S_GROUPS * S_STATE, S_GROUPS * S_STATE, S_HEADS, 2 * D_MODEL)
    cuts = [int(c) for c in np.cumsum(sizes)[:-1]]
    q, k, v, o, ig, fg, z, xs, bm, cm, dt, gate = jnp.split(w_in.astype(BF16), cuts, axis=-1)
    pad = jnp.zeros(w_in.shape[:-1] + (LANES - MISC_DT - S_HEADS,), BF16)
    return (jnp.concatenate([xs, z, gate, v, o, q, k, bm, cm], axis=-1),
            jnp.concatenate([ig, fg, dt, pad], axis=-1))


def _misc_row(depth, pieces):
    row = jnp.zeros((depth, LANES), F32)
    for off, val in pieces:
        row = row.at[:, off:off + val.shape[-1]].set(val.astype(F32))
    return row[:, None, :]


def kernel(x, p, g_ffn1, w1_ffn1, w3_ffn1, w2_ffn1, g_mix, w_in, b_igate, b_fgate, g_mlstm_head, conv_w, conv_b, dt_bias, a_log, d_skip, g_ssm_out, w_branch_a, w_branch_b, w_out, g_ffn2, w1_ffn2, w3_ffn2, w2_ffn2, g_ple, w_ple_gate, w_ple, g_final):
    bsz, s_len, _ = x.shape
    depth = w_in.shape[0]
    t = bsz * s_len
    bf = lambda a: a.astype(BF16)
    rows = lambda a: a[:, None, :]
    w1a, w3a, w2a = bf(w1_ffn1), bf(w3_ffn1), bf(w2_ffn1)
    w1b, w3b, w2b = bf(w1_ffn2), bf(w3_ffn2), bf(w2_ffn2)
    w_proj, w_misc = _reorder_w_in(w_in)
    wa, wb, wo = bf(w_branch_a), bf(w_branch_b), bf(w_out)
    wg, wp = bf(w_ple_gate), bf(w_ple)
    gate_bias = _misc_row(depth, [(MISC_I, b_igate), (MISC_F, b_fgate)])
    dtb = _misc_row(depth, [(MISC_DT, dt_bias)])
    alog = _misc_row(depth, [(MISC_DT, a_log)])
    dsk = rows(jnp.repeat(d_skip, S_HEADDIM, axis=-1))
    gn = S_GROUPS * S_STATE
    cwx, cwb, cwc = conv_w[..., :S_INNER], conv_w[..., S_INNER:S_INNER + gn], conv_w[..., S_INNER + gn:]
    conv_b = rows(conv_b)
    cbx, cbb, cbc = conv_b[..., :S_INNER], conv_b[..., S_INNER:S_INNER + gn], conv_b[..., S_INNER + gn:]
    g1, gm, g2, gp, gs = rows(g_ffn1), rows(g_mix), rows(g_ffn2), rows(g_ple), rows(g_ssm_out)

    h = x.reshape(t, D_MODEL)
    p2 = p.reshape(depth, t, PLE_DIM)
    for i in range(depth):
        h = _ffn(h, i, g1, w1a, w3a, w2a)
        proj, misc = _proj(h, i, gm, w_proj, w_misc)
        proj3 = proj.reshape(bsz, s_len, N_PROJ)
        misc3 = misc.reshape(bsz, s_len, LANES)
        ha = _mlstm(proj3, misc3, i, gate_bias, g_mlstm_head)
        yb = _ssd(proj3, misc3, i, cwx, cwb, cwc, cbx, cbb, cbc, dtb, alog, dsk, gs)
        h = _merge(h, ha.reshape(t, M_INNER), yb.reshape(t, S_INNER), proj, i, wa, wb, wo)
        h = _ffn(h, i, g2, w1b, w3b, w2b, ple=(p2, gp, wg, wp),
                 g_final=g_final[None, :] if i == depth - 1 else None)
    return h.reshape(bsz, s_len, D_MODEL)
```

```python
import functools

import numpy as np
import jax
import jax.numpy as jnp
from jax import lax
from jax.experimental import pallas as pl
from jax.experimental.pallas import tpu as pltpu

F32 = jnp.float32
BF16 = jnp.bfloat16

D_MODEL = 1024
PLE_DIM = 256
CHUNK = 128
EPS = 1e-6
M_HEADS = 4
M_QK = 128
M_V = 256
M_INNER = M_HEADS * M_V
S_INNER = 2048
S_HEADDIM = 64
S_HEADS = 32
S_GROUPS = 4
S_STATE = 128
S_CONV = 4
S_HEADS_PER_GROUP = S_HEADS // S_GROUPS
D_FF = 2816
LANES = 128
SUBLANES = 8
MXU_DIM = 256

OFF_XS = 0
OFF_Z = 2048
OFF_GATE = 4096
OFF_V = 6144
OFF_O = 7168
OFF_Q = 8192
OFF_K = 8704
OFF_B = 9216
OFF_C = 9728
N_PROJ = 10240
PROJ_TN = 1024
TILE_Z = OFF_Z // PROJ_TN
TILE_GATE = OFF_GATE // PROJ_TN
TILE_O = OFF_O // PROJ_TN
TILE_BC = OFF_B // PROJ_TN
PROJ_PAD = 16
MISC_I = 0
MISC_F = M_HEADS
MISC_DT = 2 * M_HEADS

VMEM_LIMIT = 56 * 1024 * 1024
FFN_CHUNKS = (4 * MXU_DIM, 4 * MXU_DIM, D_FF - 8 * MXU_DIM)


def _dot(a, b):
    return jnp.dot(a, b, preferred_element_type=F32)


def _dot_nt(a, b):
    return lax.dot_general(a, b, (((1,), (1,)), ((), ())), preferred_element_type=F32)


def _rms(x, g):
    r = lax.rsqrt(jnp.mean(x * x, axis=-1, keepdims=True) + EPS)
    return x * r * g


def _sigmoid(x):
    return 0.5 * jnp.tanh(0.5 * x) + 0.5


def _silu(x):
    half = 0.5 * x
    return half + half * jnp.tanh(half)


def _split3(x):
    hi = x.astype(BF16)
    r = x - hi.astype(F32)
    mid = r.astype(BF16)
    lo = (r - mid.astype(F32)).astype(BF16)
    return hi, mid, lo


def _cumsum_rows(tril_bf16, x):
    hi, mid, lo = _split3(x)
    return _dot(tril_bf16, hi) + _dot(tril_bf16, mid) + _dot(tril_bf16, lo)


def _expand_lanes(x, e_ref, parts=2):
    return _dot(jnp.concatenate(_split3(x)[:parts], axis=1), e_ref[...])


def _selection_matrix(parts, src_lanes, width):
    e = np.zeros((LANES, len(src_lanes) * width), np.float32)
    for c, src in enumerate(src_lanes):
        e[src, c * width:(c + 1) * width] = 1.0
    return jnp.asarray(np.concatenate([e] * parts, axis=0), dtype=BF16)


def _layer_spec(shape, layer):
    nd = len(shape)
    return pl.BlockSpec((None,) + tuple(shape), lambda *_: (layer,) + (0,) * nd,
                        pipeline_mode=pl.Buffered(1))


def _const_spec(shape):
    nd = len(shape)
    return pl.BlockSpec(shape, lambda *_: (0,) * nd, pipeline_mode=pl.Buffered(1))


def _ffn_kernel(*refs, has_ple, final_norm):
    h_ref, g_ref, w1_ref, w3_ref, w2_ref = refs[:5]
    pos = 5
    if has_ple:
        p_ref, gp_ref, wg_ref, wp_ref = refs[pos:pos + 4]
        pos += 4
    if final_norm:
        gf_ref = refs[pos]
        pos += 1
    o_ref = refs[pos]

    h = h_ref[...]
    u = _rms(h, g_ref[...]).astype(BF16)
    acc = None
    lo = 0
    for fc in FFN_CHUNKS:
        a = _dot(u, w1_ref[:, lo:lo + fc])
        b = _dot(u, w3_ref[:, lo:lo + fc])
        hid = (_silu(a) * b).astype(BF16)
        d = _dot(hid, w2_ref[lo:lo + fc, :])
        acc = d if acc is None else acc + d
        lo += fc
    h = h + 0.5 * acc
    if has_ple:
        ug = _rms(h, gp_ref[...]).astype(BF16)
        gate = _sigmoid(_dot(ug, wg_ref[...]))
        pe = _dot(p_ref[...].astype(BF16), wp_ref[...])
        h = h + gate * pe
    if final_norm:
        h = _rms(h, gf_ref[...])
    o_ref[...] = h


def _ffn(h, layer, g, w1, w3, w2, ple=None, g_final=None, tm=512):
    t = h.shape[0]
    tm = min(tm, t)
    row = lambda i: (i, 0)
    in_specs = [pl.BlockSpec((tm, D_MODEL), row), _layer_spec((1, D_MODEL), layer),
                _layer_spec((D_MODEL, D_FF), layer), _layer_spec((D_MODEL, D_FF), layer),
                _layer_spec((D_FF, D_MODEL), layer)]
    args = [h, g, w1, w3, w2]
    if ple is not None:
        p, gp, wg, wp = ple
        in_specs += [pl.BlockSpec((None, tm, PLE_DIM), lambda i: (layer, i, 0)),
                     _layer_spec((1, D_MODEL), layer), _layer_spec((D_MODEL, D_MODEL), layer),
                     _layer_spec((PLE_DIM, D_MODEL), layer)]
        args += [p, gp, wg, wp]
    if g_final is not None:
        in_specs.append(_const_spec((1, D_MODEL)))
        args.append(g_final)
    return pl.pallas_call(
        functools.partial(_ffn_kernel, has_ple=ple is not None, final_norm=g_final is not None),
        grid=(t // tm,),
        in_specs=in_specs,
        out_specs=pl.BlockSpec((tm, D_MODEL), row),
        out_shape=jax.ShapeDtypeStruct((t, D_MODEL), F32),
        compiler_params=pltpu.CompilerParams(dimension_semantics=("parallel",),
                                             vmem_limit_bytes=VMEM_LIMIT),
        name="ffn",
    )(*args)


def _proj_kernel(hp_ref, h_ref, g_ref, w_ref, wm_ref, cw_ref, cb_ref, o_ref, misc_ref, u_ref, *, tiles_per_seq):
    i = pl.program_id(0)
    j = pl.program_id(1)
    tm, tn = o_ref.shape

    @pl.when(j == 0)
    def _():
        u = _rms(h_ref[...], g_ref[...]).astype(BF16)
        u_ref[PROJ_PAD:, :] = u
        up = _rms(hp_ref[...], g_ref[...])
        u_ref[0:PROJ_PAD, :] = jnp.where(i % tiles_per_seq == 0, 0.0, up).astype(BF16)
        misc_ref[...] = _dot(u, wm_ref[...])

    is_conv = (j < TILE_Z) | (j == TILE_BC)
    is_silu = (j >= TILE_Z) & (j < TILE_GATE)
    is_sig = j == TILE_O

    subs = [slice(n * MXU_DIM, (n + 1) * MXU_DIM) for n in range(tn // MXU_DIM)]

    @pl.when(is_conv)
    def _():
        u = u_ref[...]
        for cs in subs:
            a = _dot(u, w_ref[:, cs])
            a1 = pltpu.roll(a, 1, axis=0)
            b = cw_ref[1:2, cs] * a + cw_ref[0:1, cs] * a1
            c = cw_ref[3:4, cs] * a + cw_ref[2:3, cs] * a1 + cb_ref[:, cs]
            y = (c + pltpu.roll(b, 2, axis=0))[PROJ_PAD:, :]
            o_ref[:, cs] = _silu(y).astype(o_ref.dtype)

    def plain_branch(cond, fn):
        @pl.when(cond)
        def _():
            u = u_ref[PROJ_PAD:, :]
            for cs in subs:
                o_ref[:, cs] = fn(_dot(u, w_ref[:, cs])).astype(o_ref.dtype)

    plain_branch(is_silu, _silu)
    plain_branch(is_sig, _sigmoid)
    plain_branch(jnp.logical_not(is_conv | is_silu | is_sig), lambda v: v)


def _proj(h, s_len, layer, g, w, w_misc, cw, cb, tm=1024):
    t = h.shape[0]
    tm = min(tm, s_len)
    tn = PROJ_TN
    pads_per_tile = tm // PROJ_PAD
    return pl.pallas_call(
        functools.partial(_proj_kernel, tiles_per_seq=s_len // tm),
        grid=(t // tm, N_PROJ // tn),
        in_specs=[pl.BlockSpec((PROJ_PAD, D_MODEL), lambda i, j: (jnp.maximum(i * pads_per_tile - 1, 0), 0)),
                  pl.BlockSpec((tm, D_MODEL), lambda i, j: (i, 0)),
                  pl.BlockSpec((None, 1, D_MODEL), lambda i, j: (layer, 0, 0)),
                  pl.BlockSpec((None, D_MODEL, tn), lambda i, j: (layer, 0, j)),
                  pl.BlockSpec((None, D_MODEL, LANES), lambda i, j: (layer, 0, 0)),
                  pl.BlockSpec((None, S_CONV, tn), lambda i, j: (layer, 0, j)),
                  pl.BlockSpec((None, 1, tn), lambda i, j: (layer, 0, j))],
        out_specs=[pl.BlockSpec((tm, tn), lambda i, j: (i, j)),
                   pl.BlockSpec((tm, LANES), lambda i, j: (i, 0))],
        out_shape=[jax.ShapeDtypeStruct((t, N_PROJ), BF16), jax.ShapeDtypeStruct((t, LANES), F32)],
        scratch_shapes=[pltpu.VMEM((tm + PROJ_PAD, D_MODEL), BF16)],
        compiler_params=pltpu.CompilerParams(dimension_semantics=("parallel", "arbitrary"),
                                             vmem_limit_bytes=VMEM_LIMIT),
        name="proj",
    )(h, h, g, w, w_misc, cw, cb)


def _mlstm_kernel(q_ref, k_ref, v_ref, o_ref, misc_ref, bias_ref, gh_ref, em_ref, out_ref,
                  c_st, n_st, m_st, cols_ref, rows_ref):
    L = CHUNK
    scale = M_QK ** -0.5
    wide = lambda a: jnp.concatenate([a] * (M_V // LANES), axis=1)

    @pl.when(pl.program_id(1) == 0)
    def _():
        c_st[...] = jnp.zeros_like(c_st)
        n_st[...] = jnp.zeros_like(n_st)
        m_st[...] = jnp.zeros_like(m_st)

    x = misc_ref[0] + bias_ref[...]
    lane = lax.broadcasted_iota(jnp.int32, (L, LANES), 1)
    logf = jnp.minimum(x, 0.0) - jnp.log1p(jnp.exp(-jnp.abs(x)))
    xg = jnp.where(lane < MISC_F, x, logf)
    t_idx = lax.broadcasted_iota(jnp.int32, (L, L), 0)
    s_idx = lax.broadcasted_iota(jnp.int32, (L, L), 1)
    causal = s_idx <= t_idx
    tril = jnp.where(causal, 1.0, 0.0).astype(BF16)
    bc = _cumsum_rows(tril, xg)
    comb = jnp.where(lane < MISC_F, xg, bc)
    rows_ref[...] = comb.T
    cols_ref[...] = _expand_lanes(comb, em_ref, parts=3)

    for h in range(M_HEADS):
        qb = q_ref[0, :, h * M_QK:(h + 1) * M_QK]
        kb = k_ref[0, :, h * M_QK:(h + 1) * M_QK]
        vb = v_ref[0, :, h * M_V:(h + 1) * M_V]
        li_col = cols_ref[:, (MISC_I + h) * LANES:(MISC_I + h + 1) * LANES]
        bc_col = cols_ref[:, (MISC_F + h) * LANES:(MISC_F + h + 1) * LANES]
        li_row = rows_ref[MISC_I + h:MISC_I + h + 1, :]
        bc_row = rows_ref[MISC_F + h:MISC_F + h + 1, :]
        gtot = bc_col[L - 1:L, :]
        m_prev = m_st[h:h + 1, :]
        n_prev = n_st[h:h + 1, :]
        c_prev = c_st[h]

        dlog = jnp.where(causal, bc_col - bc_row + li_row, -jnp.inf)
        m_inter = bc_col + m_prev
        m_t = jnp.maximum(jnp.max(dlog, axis=1, keepdims=True), m_inter)
        w_intra = jnp.exp(dlog - m_t)
        w_inter = jnp.exp(m_inter - m_t)

        sc = _dot_nt(qb, kb) * scale * w_intra
        qn = jnp.sum(qb.astype(F32) * n_prev, axis=1, keepdims=True) * scale
        den = jnp.sum(sc, axis=1, keepdims=True) + w_inter * qn
        inv = 1.0 / jnp.maximum(jnp.abs(den), jnp.exp(-m_t))
        num = _dot(sc.astype(BF16), vb) + _dot(qb, c_prev.astype(BF16)) * wide(scale * w_inter)
        hh = num * wide(inv)

        r = lax.rsqrt(jnp.mean(hh * hh, axis=1, keepdims=True) + EPS)
        og = o_ref[0, :, h * M_V:(h + 1) * M_V].astype(F32)
        out_ref[0, :, h * M_V:(h + 1) * M_V] = (hh * r * gh_ref[h:h + 1, :] * og).astype(out_ref.dtype)

        a_col = gtot - bc_col + li_col
        m_loc = jnp.max(a_col, axis=0, keepdims=True)
        kw = kb.astype(F32) * jnp.exp(a_col - m_loc)
        c_loc = _dot(kw.T.astype(BF16), vb)
        n_loc = jnp.sum(kw, axis=0, keepdims=True)
        m_new = jnp.maximum(gtot + m_prev, m_loc)
        sp = jnp.exp(gtot + m_prev - m_new)
        sl = jnp.exp(m_loc - m_new)
        c_st[h] = wide(sp) * c_prev + wide(sl) * c_loc
        n_st[h:h + 1, :] = sp * n_prev + sl * n_loc
        m_st[h:h + 1, :] = m_new


def _chunk_spec(width, off):
    return pl.BlockSpec((1, CHUNK, width), lambda b, c: (b, c, off // width))


def _mlstm(proj3, misc3, layer, gate_bias, g_head):
    bsz, s_len, _ = proj3.shape
    n_gate = 2 * M_HEADS
    em = _selection_matrix(3, range(n_gate), LANES)
    return pl.pallas_call(
        _mlstm_kernel,
        grid=(bsz, s_len // CHUNK),
        in_specs=[_chunk_spec(M_HEADS * M_QK, OFF_Q), _chunk_spec(M_HEADS * M_QK, OFF_K),
                  _chunk_spec(M_INNER, OFF_V), _chunk_spec(M_INNER, OFF_O), _chunk_spec(LANES, 0),
                  _layer_spec((1, LANES), layer), _layer_spec((M_HEADS, M_V), layer), _const_spec(em.shape)],
        out_specs=pl.BlockSpec((1, CHUNK, M_INNER), lambda b, c: (b, c, 0)),
        out_shape=jax.ShapeDtypeStruct((bsz, s_len, M_INNER), BF16),
        scratch_shapes=[pltpu.VMEM((M_HEADS, M_QK, M_V), F32),
                        pltpu.VMEM((SUBLANES, M_QK), F32),
                        pltpu.VMEM((SUBLANES, LANES), F32),
                        pltpu.VMEM((CHUNK, n_gate * LANES), F32),
                        pltpu.VMEM((LANES, CHUNK), F32)],
        compiler_params=pltpu.CompilerParams(dimension_semantics=("parallel", "arbitrary"),
                                             vmem_limit_bytes=VMEM_LIMIT),
        name="mlstm",
    )(proj3, proj3, proj3, proj3, misc3, gate_bias, g_head, em)


def _ssd_kernel(xs_ref, zs_ref, b_ref, c_ref, misc_ref, dtb_ref, alog_ref, dsk_ref, g_ref, e128_ref, e64_ref,
                out_ref, st_ref, cb128_ref, y_ref):
    L = CHUNK

    @pl.when(pl.program_id(1) == 0)
    def _():
        st_ref[...] = jnp.zeros_like(st_ref)

    lane = lax.broadcasted_iota(jnp.int32, (L, LANES), 1)
    dt_lanes = (lane >= MISC_DT) & (lane < MISC_DT + S_HEADS)
    xm = misc_ref[0] + dtb_ref[...]
    dt = jnp.maximum(xm, 0.0) + jnp.log1p(jnp.exp(-jnp.abs(xm)))
    dt = jnp.where(dt_lanes, dt, 0.0)
    a_neg = jnp.where(dt_lanes[0:1, :], -jnp.exp(alog_ref[...]), 0.0)
    t_idx = lax.broadcasted_iota(jnp.int32, (L, L), 0)
    s_idx = lax.broadcasted_iota(jnp.int32, (L, L), 1)
    causal = s_idx <= t_idx
    tril = jnp.where(causal, 1.0, 0.0).astype(BF16)
    acs = _cumsum_rows(tril, dt * a_neg)
    last = acs[L - 1:L, :]
    dtw = dt * jnp.exp(last - acs)
    adj_t = (acs - jnp.where(dt_lanes, jnp.log(dt), 0.0)).T
    cb128_ref[...] = _expand_lanes(acs, e128_ref)
    both64 = _expand_lanes(jnp.concatenate([dtw, jnp.exp(acs)], axis=0), e64_ref)
    dtw64 = both64[0:L, :]
    ex64 = both64[L:2 * L, :]
    gc64 = ex64[L - 1:L, :]
    lane_lo = lane < S_HEADDIM
    gw = S_HEADS_PER_GROUP * S_HEADDIM

    for g in range(S_GROUPS):
        bg = b_ref[0, :, g * S_STATE:(g + 1) * S_STATE]
        cg = c_ref[0, :, g * S_STATE:(g + 1) * S_STATE]
        cb = _dot_nt(cg, bg)
        st_g = st_ref[:, g * gw:(g + 1) * gw]
        y_inter = _dot(cg, st_g.astype(BF16)) * ex64[:, g * gw:(g + 1) * gw]
        for e in range(0, S_HEADS_PER_GROUP, 2):
            h0 = g * S_HEADS_PER_GROUP + e
            j = h0 // 2
            x_pair = xs_ref[0, :, j * LANES:(j + 1) * LANES]
            zero = jnp.zeros_like(x_pair)
            rhs = jnp.concatenate([jnp.where(lane_lo, x_pair, zero), jnp.where(lane_lo, zero, x_pair)], axis=0)
            ws = []
            for h in (h0, h0 + 1):
                col = cb128_ref[:, h * LANES:(h + 1) * LANES]
                row = adj_t[MISC_DT + h:MISC_DT + h + 1, :]
                ws.append(cb * jnp.exp(jnp.where(causal, col - row, -jnp.inf)))
            y_intra = _dot(jnp.concatenate(ws, axis=1).astype(BF16), rhs)
            y_ref[:, j * LANES:(j + 1) * LANES] = (
                y_intra + y_inter[:, e * S_HEADDIM:(e + 2) * S_HEADDIM]
                + x_pair.astype(F32) * dsk_ref[:, j * LANES:(j + 1) * LANES])
        xdtw = (xs_ref[0, :, g * gw:(g + 1) * gw].astype(F32) * dtw64[:, g * gw:(g + 1) * gw]).astype(BF16)
        st_ref[:, g * gw:(g + 1) * gw] = (st_g * gc64[:, g * gw:(g + 1) * gw]
                                          + _dot(bg.astype(F32).T.astype(BF16), xdtw))

    y = y_ref[...] * zs_ref[0].astype(F32)
    out_ref[0] = _rms(y, g_ref[...]).astype(out_ref.dtype)


def _ssd(proj3, misc3, layer, dtb, alog, dsk, g_out):
    bsz, s_len, _ = proj3.shape
    L = CHUNK
    gn = S_GROUPS * S_STATE
    dt_lanes = range(MISC_DT, MISC_DT + S_HEADS)
    e128 = _selection_matrix(2, dt_lanes, LANES)
    e64 = _selection_matrix(2, dt_lanes, S_HEADDIM)
    stacked = [dtb, alog, dsk, g_out]
    return pl.pallas_call(
        _ssd_kernel,
        grid=(bsz, s_len // L),
        in_specs=[_chunk_spec(S_INNER, OFF_XS), _chunk_spec(S_INNER, OFF_Z), _chunk_spec(gn, OFF_B),
                  _chunk_spec(gn, OFF_C), _chunk_spec(LANES, 0)]
                 + [_layer_spec(a.shape[1:], layer) for a in stacked]
                 + [_const_spec(e128.shape), _const_spec(e64.shape)],
        out_specs=pl.BlockSpec((1, L, S_INNER), lambda b, c: (b, c, 0)),
        out_shape=jax.ShapeDtypeStruct((bsz, s_len, S_INNER), BF16),
        scratch_shapes=[pltpu.VMEM((S_STATE, S_INNER), F32),
                        pltpu.VMEM((L, S_HEADS * LANES), F32),
                        pltpu.VMEM((L, S_INNER), F32)],
        compiler_params=pltpu.CompilerParams(dimension_semantics=("parallel", "arbitrary"),
                                             vmem_limit_bytes=VMEM_LIMIT),
        name="ssd",
    )(proj3, proj3, proj3, proj3, misc3, *stacked, e128, e64)


def _merge_kernel(h_ref, ha_ref, yb_ref, gate_ref, wa_ref, wb_ref, wo_ref, o_ref):
    gate = _sigmoid(gate_ref[...].astype(F32))
    a = _dot(ha_ref[...], wa_ref[...])
    b = _dot(yb_ref[...], wb_ref[...])
    merged = gate[:, :D_MODEL] * a + gate[:, D_MODEL:] * b
    o_ref[...] = h_ref[...] + _dot(merged.astype(BF16), wo_ref[...])


def _merge(h, ha, yb, proj, layer, wa, wb, wo, tm=512):
    t = h.shape[0]
    tm = min(tm, t)
    row = lambda i: (i, 0)
    return pl.pallas_call(
        _merge_kernel,
        grid=(t // tm,),
        in_specs=[pl.BlockSpec((tm, D_MODEL), row), pl.BlockSpec((tm, M_INNER), row),
                  pl.BlockSpec((tm, S_INNER), row),
                  pl.BlockSpec((tm, 2 * D_MODEL), lambda i: (i, OFF_GATE // (2 * D_MODEL))),
                  _layer_spec((M_INNER, D_MODEL), layer), _layer_spec((S_INNER, D_MODEL), layer),
                  _layer_spec((D_MODEL, D_MODEL), layer)],
        out_specs=pl.BlockSpec((tm, D_MODEL), row),
        out_shape=jax.ShapeDtypeStruct((t, D_MODEL), F32),
        compiler_params=pltpu.CompilerParams(dimension_semantics=("parallel",),
                                             vmem_limit_bytes=VMEM_LIMIT),
        name="merge",
    )(h, ha, yb, proj, wa, wb, wo)


def _reorder_w_in(w_in):
    sizes = (M_HEADS * M_QK, M_HEADS * M_QK, M_INNER, M_INNER, M_HEADS, M_HEADS, S_INNER,
             S_INNER, S_GROUPS * S_STATE, S_GROUPS * S_STATE, S_HEADS, 2 * D_MODEL)
    cuts = [int(c) for c in np.cumsum(sizes)[:-1]]
    q, k, v, o, ig, fg, z, xs, bm, cm, dt, gate = jnp.split(w_in.astype(BF16), cuts, axis=-1)
    pad = jnp.zeros(w_in.shape[:-1] + (LANES - MISC_DT - S_HEADS,), BF16)
    return (jnp.concatenate([xs, z, gate, v, o, q, k, bm, cm], axis=-1),
            jnp.concatenate([ig, fg, dt, pad], axis=-1))


def _misc_row(depth, pieces):
    row = jnp.zeros((depth, LANES), F32)
    for off, val in pieces:
        row = row.at[:, off:off + val.shape[-1]].set(val.astype(F32))
    return row[:, None, :]


def kernel(x, p, g_ffn1, w1_ffn1, w3_ffn1, w2_ffn1, g_mix, w_in, b_igate, b_fgate, g_mlstm_head, conv_w, conv_b, dt_bias, a_log, d_skip, g_ssm_out, w_branch_a, w_branch_b, w_out, g_ffn2, w1_ffn2, w3_ffn2, w2_ffn2, g_ple, w_ple_gate, w_ple, g_final):
    bsz, s_len, _ = x.shape
    depth = w_in.shape[0]
    t = bsz * s_len
    bf = lambda a: a.astype(BF16)
    rows = lambda a: a[:, None, :]
    w1a, w3a, w2a = bf(w1_ffn1), bf(w3_ffn1), bf(w2_ffn1)
    w1b, w3b, w2b = bf(w1_ffn2), bf(w3_ffn2), bf(w2_ffn2)
    w_proj, w_misc = _reorder_w_in(w_in)
    wa, wb, wo = bf(w_branch_a), bf(w_branch_b), bf(w_out)
    wg, wp = bf(w_ple_gate), bf(w_ple)
    gate_bias = _misc_row(depth, [(MISC_I, b_igate), (MISC_F, b_fgate)])
    dtb = _misc_row(depth, [(MISC_DT, dt_bias)])
    alog = _misc_row(depth, [(MISC_DT, a_log)])
    dsk = rows(jnp.repeat(d_skip, S_HEADDIM, axis=-1))
    conv_cols = lambda a: jnp.zeros(a.shape[:-1] + (N_PROJ,), F32).at[..., OFF_XS:OFF_XS + S_INNER].set(
        a[..., :S_INNER]).at[..., OFF_B:N_PROJ].set(a[..., S_INNER:])
    cw_cols, cb_cols = conv_cols(conv_w), conv_cols(rows(conv_b))
    g1, gm, g2, gp, gs = rows(g_ffn1), rows(g_mix), rows(g_ffn2), rows(g_ple), rows(g_ssm_out)

    h = x.reshape(t, D_MODEL)
    p2 = p.reshape(depth, t, PLE_DIM)
    for i in range(depth):
        h = _ffn(h, i, g1, w1a, w3a, w2a)
        proj, misc = _proj(h, s_len, i, gm, w_proj, w_misc, cw_cols, cb_cols)
        proj3 = proj.reshape(bsz, s_len, N_PROJ)
        misc3 = misc.reshape(bsz, s_len, LANES)
        ha = _mlstm(proj3, misc3, i, gate_bias, g_mlstm_head)
        yb = _ssd(proj3, misc3, i, dtb, alog, dsk, gs)
        h = _merge(h, ha.reshape(t, M_INNER), yb.reshape(t, S_INNER), proj, i, wa, wb, wo)
        h = _ffn(h, i, g2, w1b, w3b, w2b, ple=(p2, gp, wg, wp),
                 g_final=g_final[None, :] if i == depth - 1 else None)
    return h.reshape(bsz, s_len, D_MODEL)
```

```python
import functools

import numpy as np
import jax
import jax.numpy as jnp
from jax import lax
from jax.experimental import pallas as pl
from jax.experimental.pallas import tpu as pltpu

F32 = jnp.float32
BF16 = jnp.bfloat16

D_MODEL = 1024
PLE_DIM = 256
CHUNK = 128
EPS = 1e-6
M_HEADS = 4
M_QK = 128
M_V = 256
M_INNER = M_HEADS * M_V
S_INNER = 2048
S_HEADDIM = 64
S_HEADS = 32
S_GROUPS = 4
S_STATE = 128
S_CONV = 4
S_HEADS_PER_GROUP = S_HEADS // S_GROUPS
D_FF = 2816
LANES = 128
SUBLANES = 8
MXU_DIM = 256
LOG2_E = 1.4426950408889634

OFF_XS = 0
OFF_Z = 2048
OFF_GATE = 4096
OFF_V = 6144
OFF_O = 7168
OFF_Q = 8192
OFF_K = 8704
OFF_B = 9216
OFF_C = 9728
N_PROJ = 10240
PROJ_TN = 1024
TILE_Z = OFF_Z // PROJ_TN
TILE_GATE = OFF_GATE // PROJ_TN
TILE_O = OFF_O // PROJ_TN
TILE_BC = OFF_B // PROJ_TN
PROJ_PAD = 16
MISC_I = 0
MISC_F = M_HEADS
MISC_DT = 2 * M_HEADS

VMEM_LIMIT = 56 * 1024 * 1024
FFN_CHUNKS = (4 * MXU_DIM, 4 * MXU_DIM, D_FF - 8 * MXU_DIM)


def _dot(a, b):
    return jnp.dot(a, b, preferred_element_type=F32)


def _dot_nt(a, b):
    return lax.dot_general(a, b, (((1,), (1,)), ((), ())), preferred_element_type=F32)


def _rms(x, g):
    r = lax.rsqrt(jnp.mean(x * x, axis=-1, keepdims=True) + EPS)
    return x * r * g


def _sigmoid(x):
    return 0.5 * jnp.tanh(0.5 * x) + 0.5


def _silu(x):
    half = 0.5 * x
    return half + half * jnp.tanh(half)


def _split3(x):
    hi = x.astype(BF16)
    r = x - hi.astype(F32)
    mid = r.astype(BF16)
    lo = (r - mid.astype(F32)).astype(BF16)
    return hi, mid, lo


def _cumsum_rows(tril_bf16, x):
    hi, mid, lo = _split3(x)
    return _dot(tril_bf16, hi) + _dot(tril_bf16, mid) + _dot(tril_bf16, lo)


def _expand_lanes(x, e_ref, parts=2):
    return _dot(jnp.concatenate(_split3(x)[:parts], axis=1), e_ref[...])


def _selection_matrix(parts, src_lanes, width):
    e = np.zeros((LANES, len(src_lanes) * width), np.float32)
    for c, src in enumerate(src_lanes):
        e[src, c * width:(c + 1) * width] = 1.0
    return jnp.asarray(np.concatenate([e] * parts, axis=0), dtype=BF16)


def _layer_spec(shape, layer):
    nd = len(shape)
    return pl.BlockSpec((None,) + tuple(shape), lambda *_: (layer,) + (0,) * nd,
                        pipeline_mode=pl.Buffered(1))


def _const_spec(shape):
    nd = len(shape)
    return pl.BlockSpec(shape, lambda *_: (0,) * nd, pipeline_mode=pl.Buffered(1))


def _ffn_kernel(*refs, has_ple, final_norm):
    h_ref, g_ref, w1_ref, w3_ref, w2_ref = refs[:5]
    pos = 5
    if has_ple:
        p_ref, gp_ref, wg_ref, wp_ref = refs[pos:pos + 4]
        pos += 4
    if final_norm:
        gf_ref = refs[pos]
        pos += 1
    o_ref = refs[pos]

    h = h_ref[...]
    u = _rms(h, g_ref[...]).astype(BF16)
    acc = None
    lo = 0
    for fc in FFN_CHUNKS:
        a = _dot(u, w1_ref[:, lo:lo + fc])
        b = _dot(u, w3_ref[:, lo:lo + fc])
        hid = (_silu(a) * b).astype(BF16)
        d = _dot(hid, w2_ref[lo:lo + fc, :])
        acc = d if acc is None else acc + d
        lo += fc
    h = h + 0.5 * acc
    if has_ple:
        ug = _rms(h, gp_ref[...]).astype(BF16)
        gate = _sigmoid(_dot(ug, wg_ref[...]))
        pe = _dot(p_ref[...].astype(BF16), wp_ref[...])
        h = h + gate * pe
    if final_norm:
        h = _rms(h, gf_ref[...])
    o_ref[...] = h


def _ffn(h, layer, g, w1, w3, w2, ple=None, g_final=None, tm=512):
    t = h.shape[0]
    tm = min(tm, t)
    row = lambda i: (i, 0)
    in_specs = [pl.BlockSpec((tm, D_MODEL), row), _layer_spec((1, D_MODEL), layer),
                _layer_spec((D_MODEL, D_FF), layer), _layer_spec((D_MODEL, D_FF), layer),
                _layer_spec((D_FF, D_MODEL), layer)]
    args = [h, g, w1, w3, w2]
    if ple is not None:
        p, gp, wg, wp = ple
        in_specs += [pl.BlockSpec((None, tm, PLE_DIM), lambda i: (layer, i, 0)),
                     _layer_spec((1, D_MODEL), layer), _layer_spec((D_MODEL, D_MODEL), layer),
                     _layer_spec((PLE_DIM, D_MODEL), layer)]
        args += [p, gp, wg, wp]
    if g_final is not None:
        in_specs.append(_const_spec((1, D_MODEL)))
        args.append(g_final)
    return pl.pallas_call(
        functools.partial(_ffn_kernel, has_ple=ple is not None, final_norm=g_final is not None),
        grid=(t // tm,),
        in_specs=in_specs,
        out_specs=pl.BlockSpec((tm, D_MODEL), row),
        out_shape=jax.ShapeDtypeStruct((t, D_MODEL), F32),
        compiler_params=pltpu.CompilerParams(dimension_semantics=("parallel",),
                                             vmem_limit_bytes=VMEM_LIMIT),
        name="ffn",
    )(*args)


def _proj_kernel(hp_ref, h_ref, g_ref, w_ref, wm_ref, cw_ref, cb_ref, o_ref, misc_ref, u_ref, *, tiles_per_seq):
    i = pl.program_id(0)
    j = pl.program_id(1)
    tm, tn = o_ref.shape

    @pl.when(j == 0)
    def _():
        u = _rms(h_ref[...], g_ref[...]).astype(BF16)
        u_ref[PROJ_PAD:, :] = u
        up = _rms(hp_ref[...], g_ref[...])
        u_ref[0:PROJ_PAD, :] = jnp.where(i % tiles_per_seq == 0, 0.0, up).astype(BF16)
        misc_ref[...] = _dot(u, wm_ref[...])

    is_conv = (j < TILE_Z) | (j == TILE_BC)
    is_silu = (j >= TILE_Z) & (j < TILE_GATE)
    is_sig = j == TILE_O

    subs = [slice(n * MXU_DIM, (n + 1) * MXU_DIM) for n in range(tn // MXU_DIM)]

    @pl.when(is_conv)
    def _():
        u = u_ref[...]
        for cs in subs:
            a = _dot(u, w_ref[:, cs])
            a1 = pltpu.roll(a, 1, axis=0)
            b = cw_ref[1:2, cs] * a + cw_ref[0:1, cs] * a1
            c = cw_ref[3:4, cs] * a + cw_ref[2:3, cs] * a1 + cb_ref[:, cs]
            half = (c + pltpu.roll(b, 2, axis=0))[PROJ_PAD:, :]
            o_ref[:, cs] = (half + half * jnp.tanh(half)).astype(o_ref.dtype)

    def plain_branch(cond, fn):
        @pl.when(cond)
        def _():
            u = u_ref[PROJ_PAD:, :]
            for cs in subs:
                o_ref[:, cs] = fn(_dot(u, w_ref[:, cs])).astype(o_ref.dtype)

    plain_branch(is_silu, _silu)
    plain_branch(is_sig, _sigmoid)
    plain_branch(jnp.logical_not(is_conv | is_silu | is_sig), lambda v: v)


def _proj(h, s_len, layer, g, w, w_misc, cw, cb, tm=1024):
    t = h.shape[0]
    tm = min(tm, s_len)
    tn = PROJ_TN
    pads_per_tile = tm // PROJ_PAD
    return pl.pallas_call(
        functools.partial(_proj_kernel, tiles_per_seq=s_len // tm),
        grid=(t // tm, N_PROJ // tn),
        in_specs=[pl.BlockSpec((PROJ_PAD, D_MODEL), lambda i, j: (jnp.maximum(i * pads_per_tile - 1, 0), 0)),
                  pl.BlockSpec((tm, D_MODEL), lambda i, j: (i, 0)),
                  pl.BlockSpec((None, 1, D_MODEL), lambda i, j: (layer, 0, 0)),
                  pl.BlockSpec((None, D_MODEL, tn), lambda i, j: (layer, 0, j)),
                  pl.BlockSpec((None, D_MODEL, LANES), lambda i, j: (layer, 0, 0)),
                  pl.BlockSpec((None, S_CONV, tn), lambda i, j: (layer, 0, j)),
                  pl.BlockSpec((None, 1, tn), lambda i, j: (layer, 0, j))],
        out_specs=[pl.BlockSpec((tm, tn), lambda i, j: (i, j)),
                   pl.BlockSpec((tm, LANES), lambda i, j: (i, 0))],
        out_shape=[jax.ShapeDtypeStruct((t, N_PROJ), BF16), jax.ShapeDtypeStruct((t, LANES), F32)],
        scratch_shapes=[pltpu.VMEM((tm + PROJ_PAD, D_MODEL), BF16)],
        compiler_params=pltpu.CompilerParams(dimension_semantics=("parallel", "arbitrary"),
                                             vmem_limit_bytes=VMEM_LIMIT),
        name="proj",
    )(h, h, g, w, w_misc, cw, cb)


def _mlstm_chunk(q_ref, k_ref, v_ref, o_ref, misc_ref, bias_ref, gh_ref, em_ref, out_ref,
                 c_st, n_st, m_st, cols_ref, rows_ref):
    L = CHUNK
    scale = M_QK ** -0.5
    wide = lambda a: jnp.concatenate([a] * (M_V // LANES), axis=1)

    x = misc_ref[0] + bias_ref[...]
    lane = lax.broadcasted_iota(jnp.int32, (L, LANES), 1)
    logf = jnp.minimum(x, 0.0) - jnp.log1p(jnp.exp(-jnp.abs(x)))
    xg = jnp.where(lane < MISC_F, x, logf)
    t_idx = lax.broadcasted_iota(jnp.int32, (L, L), 0)
    s_idx = lax.broadcasted_iota(jnp.int32, (L, L), 1)
    causal = s_idx <= t_idx
    tril = jnp.where(causal, 1.0, 0.0).astype(BF16)
    bc = _cumsum_rows(tril, xg)
    comb = jnp.where(lane < MISC_F, xg, bc)
    rows_ref[...] = comb.T
    cols_ref[...] = _expand_lanes(comb, em_ref, parts=3)

    for h in range(M_HEADS):
        qb = q_ref[0, :, h * M_QK:(h + 1) * M_QK]
        kb = k_ref[0, :, h * M_QK:(h + 1) * M_QK]
        vb = v_ref[0, :, h * M_V:(h + 1) * M_V]
        li_col = cols_ref[:, (MISC_I + h) * LANES:(MISC_I + h + 1) * LANES]
        bc_col = cols_ref[:, (MISC_F + h) * LANES:(MISC_F + h + 1) * LANES]
        li_row = rows_ref[MISC_I + h:MISC_I + h + 1, :]
        bc_row = rows_ref[MISC_F + h:MISC_F + h + 1, :]
        gtot = bc_col[L - 1:L, :]
        m_prev = m_st[h:h + 1, :]
        n_prev = n_st[h:h + 1, :]
        c_prev = c_st[h]

        dlog = jnp.where(causal, bc_col - bc_row + li_row, -jnp.inf)
        m_inter = bc_col + m_prev
        m_t = jnp.maximum(jnp.max(dlog, axis=1, keepdims=True), m_inter)
        w_intra = jnp.exp(dlog - m_t)
        w_inter = jnp.exp(m_inter - m_t)

        sc = _dot_nt(qb, kb) * scale * w_intra
        qn = jnp.sum(qb.astype(F32) * n_prev, axis=1, keepdims=True) * scale
        den = jnp.sum(sc, axis=1, keepdims=True) + w_inter * qn
        inv = 1.0 / jnp.maximum(jnp.abs(den), jnp.exp(-m_t))
        num = _dot(sc.astype(BF16), vb) + _dot(qb, c_prev.astype(BF16)) * wide(scale * w_inter)
        hh = num * wide(inv)

        r = lax.rsqrt(jnp.mean(hh * hh, axis=1, keepdims=True) + EPS)
        og = o_ref[0, :, h * M_V:(h + 1) * M_V].astype(F32)
        out_ref[0, :, h * M_V:(h + 1) * M_V] = (hh * r * gh_ref[h:h + 1, :] * og).astype(out_ref.dtype)

        a_col = gtot - bc_col + li_col
        m_loc = jnp.max(a_col, axis=0, keepdims=True)
        kw = kb.astype(F32) * jnp.exp(a_col - m_loc)
        c_loc = _dot(kw.T.astype(BF16), vb)
        n_loc = jnp.sum(kw, axis=0, keepdims=True)
        m_new = jnp.maximum(gtot + m_prev, m_loc)
        sp = jnp.exp(gtot + m_prev - m_new)
        sl = jnp.exp(m_loc - m_new)
        c_st[h] = wide(sp) * c_prev + wide(sl) * c_loc
        n_st[h:h + 1, :] = sp * n_prev + sl * n_loc
        m_st[h:h + 1, :] = m_new


def _chunk_spec(width, off):
    return pl.BlockSpec((1, CHUNK, width), lambda b, c: (b, c, off // width))


def _ssd_chunk(xs_ref, zs_ref, b_ref, c_ref, misc_ref, dtb_ref, alog_ref, dsk_ref, g_ref, e128_ref, e64_ref,
               out_ref, st_ref, y_ref):
    L = CHUNK

    lane = lax.broadcasted_iota(jnp.int32, (L, LANES), 1)
    dt_lanes = (lane >= MISC_DT) & (lane < MISC_DT + S_HEADS)
    xm = misc_ref[0] + dtb_ref[...]
    dt = jnp.maximum(xm, 0.0) + jnp.log1p(jnp.exp(-jnp.abs(xm)))
    dt = jnp.where(dt_lanes, dt, 0.0)
    a_neg = jnp.where(dt_lanes[0:1, :], -jnp.exp(alog_ref[...]), 0.0)
    t_idx = lax.broadcasted_iota(jnp.int32, (L, L), 0)
    s_idx = lax.broadcasted_iota(jnp.int32, (L, L), 1)
    causal = s_idx <= t_idx
    tril = jnp.where(causal, 1.0, 0.0).astype(BF16)
    acs = _cumsum_rows(tril, dt * a_neg)
    last = acs[L - 1:L, :]
    dtw = dt * jnp.exp(last - acs)
    acs2 = acs * LOG2_E
    adj_t = (acs2 - jnp.where(dt_lanes, jnp.log2(dt), 0.0)).T
    acs_parts = jnp.concatenate(_split3(acs2)[:2], axis=1)
    both_parts = jnp.concatenate(_split3(jnp.concatenate([dtw, jnp.exp(acs)], axis=0))[:2], axis=1)
    lane_lo = lane < S_HEADDIM
    gw = S_HEADS_PER_GROUP * S_HEADDIM

    for g in range(S_GROUPS):
        bg = b_ref[0, :, g * S_STATE:(g + 1) * S_STATE]
        cg = c_ref[0, :, g * S_STATE:(g + 1) * S_STATE]
        cb = _dot_nt(cg, bg)
        cols = _dot(acs_parts, e128_ref[:, g * S_HEADS_PER_GROUP * LANES:(g + 1) * S_HEADS_PER_GROUP * LANES])
        both = _dot(both_parts, e64_ref[:, g * gw:(g + 1) * gw])
        dtw64 = both[0:L, :]
        ex64 = both[L:2 * L, :]
        st_g = st_ref[:, g * gw:(g + 1) * gw]
        y_inter = _dot(cg, st_g.astype(BF16)) * ex64
        for e in range(0, S_HEADS_PER_GROUP, 2):
            h0 = g * S_HEADS_PER_GROUP + e
            j = h0 // 2
            x_pair = xs_ref[0, :, j * LANES:(j + 1) * LANES]
            zero = jnp.zeros_like(x_pair)
            rhs = jnp.concatenate([jnp.where(lane_lo, x_pair, zero), jnp.where(lane_lo, zero, x_pair)], axis=0)
            ws = []
            for k in (e, e + 1):
                col = cols[:, k * LANES:(k + 1) * LANES]
                row = adj_t[MISC_DT + h0 - e + k:MISC_DT + h0 - e + k + 1, :]
                ws.append(cb * jnp.exp2(jnp.where(causal, col - row, -jnp.inf)))
            y_intra = _dot(jnp.concatenate(ws, axis=1).astype(BF16), rhs)
            y_ref[:, j * LANES:(j + 1) * LANES] = (
                y_intra + y_inter[:, e * S_HEADDIM:(e + 2) * S_HEADDIM]
                + x_pair.astype(F32) * dsk_ref[:, j * LANES:(j + 1) * LANES])
        xdtw = (xs_ref[0, :, g * gw:(g + 1) * gw].astype(F32) * dtw64).astype(BF16)
        st_ref[:, g * gw:(g + 1) * gw] = st_g * ex64[L - 1:L, :] + _dot(bg.astype(F32).T.astype(BF16), xdtw)

    y = y_ref[...] * zs_ref[0].astype(F32)
    out_ref[0] = _rms(y, g_ref[...]).astype(out_ref.dtype)


N_MLSTM_IN, N_SSD_IN = 8, 11


def _mixer_kernel(*refs):
    m_in = refs[:N_MLSTM_IN]
    s_in = refs[N_MLSTM_IN:N_MLSTM_IN + N_SSD_IN]
    ha_ref, yb_ref = refs[N_MLSTM_IN + N_SSD_IN:N_MLSTM_IN + N_SSD_IN + 2]
    c_st, n_st, m_st, cols_ref, rows_ref, st_ref, y_ref = refs[N_MLSTM_IN + N_SSD_IN + 2:]

    @pl.when(pl.program_id(1) == 0)
    def _():
        for ref in (c_st, n_st, m_st, st_ref):
            ref[...] = jnp.zeros_like(ref)

    _mlstm_chunk(*m_in, ha_ref, c_st, n_st, m_st, cols_ref, rows_ref)
    _ssd_chunk(*s_in, yb_ref, st_ref, y_ref)


def _mixers(proj3, misc3, layer, gate_bias, g_head, dtb, alog, dsk, g_out):
    bsz, s_len, _ = proj3.shape
    L = CHUNK
    gn = S_GROUPS * S_STATE
    n_gate = 2 * M_HEADS
    em = _selection_matrix(3, range(n_gate), LANES)
    dt_lanes = range(MISC_DT, MISC_DT + S_HEADS)
    e128 = _selection_matrix(2, dt_lanes, LANES)
    e64 = _selection_matrix(2, dt_lanes, S_HEADDIM)
    out_spec = lambda width: pl.BlockSpec((1, L, width), lambda b, c: (b, c, 0))
    mlstm_specs = [_chunk_spec(M_HEADS * M_QK, OFF_Q), _chunk_spec(M_HEADS * M_QK, OFF_K),
                   _chunk_spec(M_INNER, OFF_V), _chunk_spec(M_INNER, OFF_O), _chunk_spec(LANES, 0),
                   _layer_spec((1, LANES), layer), _layer_spec((M_HEADS, M_V), layer), _const_spec(em.shape)]
    stacked = [dtb, alog, dsk, g_out]
    ssd_specs = ([_chunk_spec(S_INNER, OFF_XS), _chunk_spec(S_INNER, OFF_Z), _chunk_spec(gn, OFF_B),
                  _chunk_spec(gn, OFF_C), _chunk_spec(LANES, 0)]
                 + [_layer_spec(a.shape[1:], layer) for a in stacked]
                 + [_const_spec(e128.shape), _const_spec(e64.shape)])
    assert len(mlstm_specs) == N_MLSTM_IN and len(ssd_specs) == N_SSD_IN
    return pl.pallas_call(
        _mixer_kernel,
        grid=(bsz, s_len // L),
        in_specs=mlstm_specs + ssd_specs,
        out_specs=[out_spec(M_INNER), out_spec(S_INNER)],
        out_shape=[jax.ShapeDtypeStruct((bsz, s_len, M_INNER), BF16),
                   jax.ShapeDtypeStruct((bsz, s_len, S_INNER), BF16)],
        scratch_shapes=[pltpu.VMEM((M_HEADS, M_QK, M_V), F32),
                        pltpu.VMEM((SUBLANES, M_QK), F32),
                        pltpu.VMEM((SUBLANES, LANES), F32),
                        pltpu.VMEM((L, n_gate * LANES), F32),
                        pltpu.VMEM((LANES, L), F32),
                        pltpu.VMEM((S_STATE, S_INNER), F32),
                        pltpu.VMEM((L, S_INNER), F32)],
        compiler_params=pltpu.CompilerParams(dimension_semantics=("parallel", "arbitrary"),
                                             vmem_limit_bytes=VMEM_LIMIT),
        name="mixers",
    )(proj3, proj3, proj3, proj3, misc3, gate_bias, g_head, em,
      proj3, proj3, proj3, proj3, misc3, *stacked, e128, e64)


def _merge_kernel(h_ref, ha_ref, yb_ref, gate_ref, wa_ref, wb_ref, wo_ref, o_ref):
    gate = _sigmoid(gate_ref[...].astype(F32))
    a = _dot(ha_ref[...], wa_ref[...])
    b = _dot(yb_ref[...], wb_ref[...])
    merged = gate[:, :D_MODEL] * a + gate[:, D_MODEL:] * b
    o_ref[...] = h_ref[...] + _dot(merged.astype(BF16), wo_ref[...])


def _merge(h, ha, yb, proj, layer, wa, wb, wo, tm=512):
    t = h.shape[0]
    tm = min(tm, t)
    row = lambda i: (i, 0)
    return pl.pallas_call(
        _merge_kernel,
        grid=(t // tm,),
        in_specs=[pl.BlockSpec((tm, D_MODEL), row), pl.BlockSpec((tm, M_INNER), row),
                  pl.BlockSpec((tm, S_INNER), row),
                  pl.BlockSpec((tm, 2 * D_MODEL), lambda i: (i, OFF_GATE // (2 * D_MODEL))),
                  _layer_spec((M_INNER, D_MODEL), layer), _layer_spec((S_INNER, D_MODEL), layer),
                  _layer_spec((D_MODEL, D_MODEL), layer)],
        out_specs=pl.BlockSpec((tm, D_MODEL), row),
        out_shape=jax.ShapeDtypeStruct((t, D_MODEL), F32),
        compiler_params=pltpu.CompilerParams(dimension_semantics=("parallel",),
                                             vmem_limit_bytes=VMEM_LIMIT),
        name="merge",
    )(h, ha, yb, proj, wa, wb, wo)


def _reorder_w_in(w_in):
    sizes = (M_HEADS * M_QK, M_HEADS * M_QK, M_INNER, M_INNER, M_HEADS, M_HEADS, S_INNER,
             S_INNER, S_GROUPS * S_STATE, S_GROUPS * S_STATE, S_HEADS, 2 * D_MODEL)
    cuts = [int(c) for c in np.cumsum(sizes)[:-1]]
    q, k, v, o, ig, fg, z, xs, bm, cm, dt, gate = jnp.split(w_in.astype(BF16), cuts, axis=-1)
    pad = jnp.zeros(w_in.shape[:-1] + (LANES - MISC_DT - S_HEADS,), BF16)
    return (jnp.concatenate([xs, z, gate, v, o, q, k, bm, cm], axis=-1),
            jnp.concatenate([ig, fg, dt, pad], axis=-1))


def _misc_row(depth, pieces):
    row = jnp.zeros((depth, LANES), F32)
    for off, val in pieces:
        row = row.at[:, off:off + val.shape[-1]].set(val.astype(F32))
    return row[:, None, :]


def kernel(x, p, g_ffn1, w1_ffn1, w3_ffn1, w2_ffn1, g_mix, w_in, b_igate, b_fgate, g_mlstm_head, conv_w, conv_b, dt_bias, a_log, d_skip, g_ssm_out, w_branch_a, w_branch_b, w_out, g_ffn2, w1_ffn2, w3_ffn2, w2_ffn2, g_ple, w_ple_gate, w_ple, g_final):
    bsz, s_len, _ = x.shape
    depth = w_in.shape[0]
    t = bsz * s_len
    bf = lambda a: a.astype(BF16)
    rows = lambda a: a[:, None, :]
    w1a, w3a, w2a = bf(w1_ffn1), bf(w3_ffn1), bf(w2_ffn1)
    w1b, w3b, w2b = bf(w1_ffn2), bf(w3_ffn2), bf(w2_ffn2)
    w_proj, w_misc = _reorder_w_in(w_in)
    wa, wb, wo = bf(w_branch_a), bf(w_branch_b), bf(w_out)
    wg, wp = bf(w_ple_gate), bf(w_ple)
    gate_bias = _misc_row(depth, [(MISC_I, b_igate), (MISC_F, b_fgate)])
    dtb = _misc_row(depth, [(MISC_DT, dt_bias)])
    alog = _misc_row(depth, [(MISC_DT, a_log)])
    dsk = rows(jnp.repeat(d_skip, S_HEADDIM, axis=-1))
    conv_cols = lambda a: jnp.zeros(a.shape[:-1] + (N_PROJ,), F32).at[..., OFF_XS:OFF_XS + S_INNER].set(
        0.5 * a[..., :S_INNER]).at[..., OFF_B:N_PROJ].set(0.5 * a[..., S_INNER:])
    cw_cols, cb_cols = conv_cols(conv_w), conv_cols(rows(conv_b))
    g1, gm, g2, gp, gs = rows(g_ffn1), rows(g_mix), rows(g_ffn2), rows(g_ple), rows(g_ssm_out)

    h = x.reshape(t, D_MODEL)
    p2 = p.reshape(depth, t, PLE_DIM)
    for i in range(depth):
        h = _ffn(h, i, g1, w1a, w3a, w2a)
        proj, misc = _proj(h, s_len, i, gm, w_proj, w_misc, cw_cols, cb_cols)
        proj3 = proj.reshape(bsz, s_len, N_PROJ)
        misc3 = misc.reshape(bsz, s_len, LANES)
        ha, yb = _mixers(proj3, misc3, i, gate_bias, g_mlstm_head, dtb, alog, dsk, gs)
        h = _merge(h, ha.reshape(t, M_INNER), yb.reshape(t, S_INNER), proj, i, wa, wb, wo)
        h = _ffn(h, i, g2, w1b, w3b, w2b, ple=(p2, gp, wg, wp),
                 g_final=g_final[None, :] if i == depth - 1 else None)
    return h.reshape(bsz, s_len, D_MODEL)
```

```python
import functools

import numpy as np
import jax
import jax.numpy as jnp
from jax import lax
from jax.experimental import pallas as pl
from jax.experimental.pallas import tpu as pltpu

F32 = jnp.float32
BF16 = jnp.bfloat16

D_MODEL = 1024
PLE_DIM = 256
CHUNK = 128
EPS = 1e-6
M_HEADS = 4
M_QK = 128
M_V = 256
M_INNER = M_HEADS * M_V
S_INNER = 2048
S_HEADDIM = 64
S_HEADS = 32
S_GROUPS = 4
S_STATE = 128
S_CONV = 4
S_HEADS_PER_GROUP = S_HEADS // S_GROUPS
D_FF = 2816
LANES = 128
SUBLANES = 8
MXU_DIM = 256
LOG2_E = 1.4426950408889634

PROJ_HALF = 1024
PROJ_TN = 2 * PROJ_HALF
OFF_XS0 = 0
OFF_GATE_A = 1024
OFF_XS1 = 2048
OFF_GATE_B = 3072
OFF_B = 4096
OFF_C = 4608
OFF_V = 5120
OFF_Z = 6144
OFF_O = 8192
OFF_Q = 9216
OFF_K = 9728
N_PROJ = 10240
N_CONV_STEPS = 3
STEP_Z = 3
STEP_O = 4
PROJ_PAD = 16
MISC_I = 0
MISC_F = M_HEADS
MISC_DT = 2 * M_HEADS

VMEM_LIMIT = 56 * 1024 * 1024
FFN_CHUNKS = (4 * MXU_DIM, 4 * MXU_DIM, D_FF - 8 * MXU_DIM)


def _dot(a, b):
    return jnp.dot(a, b, preferred_element_type=F32)


def _dot_nt(a, b):
    return lax.dot_general(a, b, (((1,), (1,)), ((), ())), preferred_element_type=F32)


def _rms(x, g):
    r = lax.rsqrt(jnp.mean(x * x, axis=-1, keepdims=True) + EPS)
    return x * r * g


def _sigmoid(x):
    return 0.5 * jnp.tanh(0.5 * x) + 0.5


def _silu(x):
    half = 0.5 * x
    return half + half * jnp.tanh(half)


def _split3(x):
    hi = x.astype(BF16)
    r = x - hi.astype(F32)
    mid = r.astype(BF16)
    lo = (r - mid.astype(F32)).astype(BF16)
    return hi, mid, lo


def _cumsum_rows(tril_bf16, x):
    hi, mid, lo = _split3(x)
    return _dot(tril_bf16, hi) + _dot(tril_bf16, mid) + _dot(tril_bf16, lo)


def _expand_lanes(x, e_ref, parts=2):
    return _dot(jnp.concatenate(_split3(x)[:parts], axis=1), e_ref[...])


def _selection_matrix(parts, src_lanes, width):
    e = np.zeros((LANES, len(src_lanes) * width), np.float32)
    for c, src in enumerate(src_lanes):
        e[src, c * width:(c + 1) * width] = 1.0
    return jnp.asarray(np.concatenate([e] * parts, axis=0), dtype=BF16)


def _layer_spec(shape, layer):
    nd = len(shape)
    return pl.BlockSpec((None,) + tuple(shape), lambda *_: (layer,) + (0,) * nd,
                        pipeline_mode=pl.Buffered(1))


def _const_spec(shape):
    nd = len(shape)
    return pl.BlockSpec(shape, lambda *_: (0,) * nd, pipeline_mode=pl.Buffered(1))


def _ffn_kernel(*refs, has_ple, final_norm):
    h_ref, g_ref, w1_ref, w3_ref, w2_ref = refs[:5]
    pos = 5
    if has_ple:
        p_ref, gp_ref, wg_ref, wp_ref = refs[pos:pos + 4]
        pos += 4
    if final_norm:
        gf_ref = refs[pos]
        pos += 1
    o_ref = refs[pos]

    h = h_ref[...]
    u = _rms(h, g_ref[...]).astype(BF16)
    acc = None
    lo = 0
    for fc in FFN_CHUNKS:
        a = _dot(u, w1_ref[:, lo:lo + fc])
        b = _dot(u, w3_ref[:, lo:lo + fc])
        hid = (_silu(a) * b).astype(BF16)
        d = _dot(hid, w2_ref[lo:lo + fc, :])
        acc = d if acc is None else acc + d
        lo += fc
    h = h + 0.5 * acc
    if has_ple:
        ug = _rms(h, gp_ref[...]).astype(BF16)
        gate = _sigmoid(_dot(ug, wg_ref[...]))
        pe = _dot(p_ref[...].astype(BF16), wp_ref[...])
        h = h + gate * pe
    if final_norm:
        h = _rms(h, gf_ref[...])
    o_ref[...] = h


def _ffn(h, layer, g, w1, w3, w2, ple=None, g_final=None, tm=512):
    t = h.shape[0]
    tm = min(tm, t)
    row = lambda i: (i, 0)
    in_specs = [pl.BlockSpec((tm, D_MODEL), row), _layer_spec((1, D_MODEL), layer),
                _layer_spec((D_MODEL, D_FF), layer), _layer_spec((D_MODEL, D_FF), layer),
                _layer_spec((D_FF, D_MODEL), layer)]
    args = [h, g, w1, w3, w2]
    if ple is not None:
        p, gp, wg, wp = ple
        in_specs += [pl.BlockSpec((None, tm, PLE_DIM), lambda i: (layer, i, 0)),
                     _layer_spec((1, D_MODEL), layer), _layer_spec((D_MODEL, D_MODEL), layer),
                     _layer_spec((PLE_DIM, D_MODEL), layer)]
        args += [p, gp, wg, wp]
    if g_final is not None:
        in_specs.append(_const_spec((1, D_MODEL)))
        args.append(g_final)
    return pl.pallas_call(
        functools.partial(_ffn_kernel, has_ple=ple is not None, final_norm=g_final is not None),
        grid=(t // tm,),
        in_specs=in_specs,
        out_specs=pl.BlockSpec((tm, D_MODEL), row),
        out_shape=jax.ShapeDtypeStruct((t, D_MODEL), F32),
        compiler_params=pltpu.CompilerParams(dimension_semantics=("parallel",),
                                             vmem_limit_bytes=VMEM_LIMIT),
        name="ffn",
    )(*args)


def _proj_kernel(hp_ref, h_ref, g_ref, w_ref, wm_ref, cw_ref, cb_ref, o_ref, misc_ref, u_ref, *, tiles_per_seq):
    i = pl.program_id(0)
    j = pl.program_id(1)
    tm = o_ref.shape[0]

    @pl.when(j == 0)
    def _():
        u = _rms(h_ref[...], g_ref[...]).astype(BF16)
        u_ref[PROJ_PAD:, :] = u
        up = _rms(hp_ref[...], g_ref[...])
        u_ref[0:PROJ_PAD, :] = jnp.where(i % tiles_per_seq == 0, 0.0, up).astype(BF16)
        misc_ref[...] = _dot(u, wm_ref[...])

    n_sub = PROJ_HALF // MXU_DIM
    sub = lambda half, n: slice(half * PROJ_HALF + n * MXU_DIM, half * PROJ_HALF + (n + 1) * MXU_DIM)

    def conv_sub(cs):
        a = _dot(u_ref[...], w_ref[:, cs])
        a1 = pltpu.roll(a, 1, axis=0)
        b = cw_ref[1:2, cs] * a + cw_ref[0:1, cs] * a1
        c = cw_ref[3:4, cs] * a + cw_ref[2:3, cs] * a1 + cb_ref[:, cs]
        half = (c + pltpu.roll(b, 2, axis=0))[PROJ_PAD:, :]
        o_ref[:, cs] = (half + half * jnp.tanh(half)).astype(o_ref.dtype)

    def plain_sub(cs, fn):
        o_ref[:, cs] = fn(_dot(u_ref[PROJ_PAD:, :], w_ref[:, cs])).astype(o_ref.dtype)

    @pl.when(j < N_CONV_STEPS)
    def _():
        for n in range(n_sub):
            conv_sub(sub(0, n))
            plain_sub(sub(1, n), lambda v: v)

    @pl.when(j == STEP_Z)
    def _():
        for n in range(n_sub):
            plain_sub(sub(0, n), _silu)
            plain_sub(sub(1, n), _silu)

    @pl.when(j == STEP_O)
    def _():
        for n in range(n_sub):
            plain_sub(sub(0, n), _sigmoid)
            plain_sub(sub(1, n), lambda v: v)


def _proj(h, s_len, layer, g, w, w_misc, cw, cb, tm=1024):
    t = h.shape[0]
    tm = min(tm, s_len)
    tn = PROJ_TN
    pads_per_tile = tm // PROJ_PAD
    return pl.pallas_call(
        functools.partial(_proj_kernel, tiles_per_seq=s_len // tm),
        grid=(t // tm, N_PROJ // tn),
        in_specs=[pl.BlockSpec((PROJ_PAD, D_MODEL), lambda i, j: (jnp.maximum(i * pads_per_tile - 1, 0), 0)),
                  pl.BlockSpec((tm, D_MODEL), lambda i, j: (i, 0)),
                  pl.BlockSpec((None, 1, D_MODEL), lambda i, j: (layer, 0, 0)),
                  pl.BlockSpec((None, D_MODEL, tn), lambda i, j: (layer, 0, j)),
                  pl.BlockSpec((None, D_MODEL, LANES), lambda i, j: (layer, 0, 0)),
                  pl.BlockSpec((None, S_CONV, tn), lambda i, j: (layer, 0, j)),
                  pl.BlockSpec((None, 1, tn), lambda i, j: (layer, 0, j))],
        out_specs=[pl.BlockSpec((tm, tn), lambda i, j: (i, j)),
                   pl.BlockSpec((tm, LANES), lambda i, j: (i, 0))],
        out_shape=[jax.ShapeDtypeStruct((t, N_PROJ), BF16), jax.ShapeDtypeStruct((t, LANES), F32)],
        scratch_shapes=[pltpu.VMEM((tm + PROJ_PAD, D_MODEL), BF16)],
        compiler_params=pltpu.CompilerParams(dimension_semantics=("parallel", "arbitrary"),
                                             vmem_limit_bytes=VMEM_LIMIT),
        name="proj",
    )(h, h, g, w, w_misc, cw, cb)


def _mlstm_chunk(q_ref, k_ref, v_ref, o_ref, misc_ref, bias_ref, gh_ref, em_ref, out_ref,
                 c_st, n_st, m_st, cols_ref, rows_ref):
    L = CHUNK
    scale = M_QK ** -0.5
    wide = lambda a: jnp.concatenate([a] * (M_V // LANES), axis=1)

    x = misc_ref[0] + bias_ref[...]
    lane = lax.broadcasted_iota(jnp.int32, (L, LANES), 1)
    logf = jnp.minimum(x, 0.0) - jnp.log1p(jnp.exp(-jnp.abs(x)))
    xg = jnp.where(lane < MISC_F, x, logf)
    t_idx = lax.broadcasted_iota(jnp.int32, (L, L), 0)
    s_idx = lax.broadcasted_iota(jnp.int32, (L, L), 1)
    causal = s_idx <= t_idx
    tril = jnp.where(causal, 1.0, 0.0).astype(BF16)
    bc = _cumsum_rows(tril, xg)
    comb = jnp.where(lane < MISC_F, xg, bc)
    rows_ref[...] = comb.T
    cols_ref[...] = _expand_lanes(comb, em_ref, parts=3)

    for h in range(M_HEADS):
        qb = q_ref[0, :, h * M_QK:(h + 1) * M_QK]
        kb = k_ref[0, :, h * M_QK:(h + 1) * M_QK]
        vb = v_ref[0, :, h * M_V:(h + 1) * M_V]
        li_col = cols_ref[:, (MISC_I + h) * LANES:(MISC_I + h + 1) * LANES]
        bc_col = cols_ref[:, (MISC_F + h) * LANES:(MISC_F + h + 1) * LANES]
        li_row = rows_ref[MISC_I + h:MISC_I + h + 1, :]
        bc_row = rows_ref[MISC_F + h:MISC_F + h + 1, :]
        gtot = bc_col[L - 1:L, :]
        m_prev = m_st[h:h + 1, :]
        n_prev = n_st[h:h + 1, :]
        c_prev = c_st[h]

        dlog = jnp.where(causal, bc_col - bc_row + li_row, -jnp.inf)
        m_inter = bc_col + m_prev
        m_t = jnp.maximum(jnp.max(dlog, axis=1, keepdims=True), m_inter)
        w_intra = jnp.exp(dlog - m_t)
        w_inter = jnp.exp(m_inter - m_t)

        sc = _dot_nt(qb, kb) * scale * w_intra
        qn = jnp.sum(qb.astype(F32) * n_prev, axis=1, keepdims=True) * scale
        den = jnp.sum(sc, axis=1, keepdims=True) + w_inter * qn
        inv = 1.0 / jnp.maximum(jnp.abs(den), jnp.exp(-m_t))
        num = _dot(sc.astype(BF16), vb) + _dot(qb, c_prev.astype(BF16)) * wide(scale * w_inter)
        hh = num * wide(inv)

        r = lax.rsqrt(jnp.mean(hh * hh, axis=1, keepdims=True) + EPS)
        og = o_ref[0, :, h * M_V:(h + 1) * M_V].astype(F32)
        out_ref[0, :, h * M_V:(h + 1) * M_V] = (hh * r * gh_ref[h:h + 1, :] * og).astype(out_ref.dtype)

        a_col = gtot - bc_col + li_col
        m_loc = jnp.max(a_col, axis=0, keepdims=True)
        kw = kb.astype(F32) * jnp.exp(a_col - m_loc)
        c_loc = _dot(kw.T.astype(BF16), vb)
        n_loc = jnp.sum(kw, axis=0, keepdims=True)
        m_new = jnp.maximum(gtot + m_prev, m_loc)
        sp = jnp.exp(gtot + m_prev - m_new)
        sl = jnp.exp(m_loc - m_new)
        c_st[h] = wide(sp) * c_prev + wide(sl) * c_loc
        n_st[h:h + 1, :] = sp * n_prev + sl * n_loc
        m_st[h:h + 1, :] = m_new


def _chunk_spec(width, off):
    return pl.BlockSpec((1, CHUNK, width), lambda b, c: (b, c, off // width))


def _ssd_chunk(xs0_ref, xs1_ref, zs_ref, b_ref, c_ref, misc_ref, dtb_ref, alog_ref, dsk_ref, g_ref, e128_ref,
               e64_ref, out_ref, st_ref, y_ref):
    L = CHUNK
    xs_halves = (xs0_ref, xs1_ref)
    pairs_per_half = PROJ_HALF // LANES
    groups_per_half = S_GROUPS // 2

    lane = lax.broadcasted_iota(jnp.int32, (L, LANES), 1)
    dt_lanes = (lane >= MISC_DT) & (lane < MISC_DT + S_HEADS)
    xm = misc_ref[0] + dtb_ref[...]
    dt = jnp.maximum(xm, 0.0) + jnp.log1p(jnp.exp(-jnp.abs(xm)))
    dt = jnp.where(dt_lanes, dt, 0.0)
    a_neg = jnp.where(dt_lanes[0:1, :], -jnp.exp(alog_ref[...]), 0.0)
    t_idx = lax.broadcasted_iota(jnp.int32, (L, L), 0)
    s_idx = lax.broadcasted_iota(jnp.int32, (L, L), 1)
    causal = s_idx <= t_idx
    tril = jnp.where(causal, 1.0, 0.0).astype(BF16)
    acs = _cumsum_rows(tril, dt * a_neg)
    last = acs[L - 1:L, :]
    dtw = dt * jnp.exp(last - acs)
    acs2 = acs * LOG2_E
    adj_t = (acs2 - jnp.where(dt_lanes, jnp.log2(dt), 0.0)).T
    acs_parts = jnp.concatenate(_split3(acs2)[:2], axis=1)
    both_parts = jnp.concatenate(_split3(jnp.concatenate([dtw, jnp.exp(acs)], axis=0))[:2], axis=1)
    lane_lo = lane < S_HEADDIM
    gw = S_HEADS_PER_GROUP * S_HEADDIM

    for g in range(S_GROUPS):
        bg = b_ref[0, :, g * S_STATE:(g + 1) * S_STATE]
        cg = c_ref[0, :, g * S_STATE:(g + 1) * S_STATE]
        cb = _dot_nt(cg, bg)
        cols = _dot(acs_parts, e128_ref[:, g * S_HEADS_PER_GROUP * LANES:(g + 1) * S_HEADS_PER_GROUP * LANES])
        both = _dot(both_parts, e64_ref[:, g * gw:(g + 1) * gw])
        dtw64 = both[0:L, :]
        ex64 = both[L:2 * L, :]
        st_g = st_ref[:, g * gw:(g + 1) * gw]
        y_inter = _dot(cg, st_g.astype(BF16)) * ex64
        for e in range(0, S_HEADS_PER_GROUP, 2):
            h0 = g * S_HEADS_PER_GROUP + e
            j = h0 // 2
            jl = j % pairs_per_half
            x_pair = xs_halves[j // pairs_per_half][0, :, jl * LANES:(jl + 1) * LANES]
            zero = jnp.zeros_like(x_pair)
            rhs = jnp.concatenate([jnp.where(lane_lo, x_pair, zero), jnp.where(lane_lo, zero, x_pair)], axis=0)
            ws = []
            for k in (e, e + 1):
                col = cols[:, k * LANES:(k + 1) * LANES]
                row = adj_t[MISC_DT + h0 - e + k:MISC_DT + h0 - e + k + 1, :]
                ws.append(cb * jnp.exp2(jnp.where(causal, col - row, -jnp.inf)))
            y_intra = _dot(jnp.concatenate(ws, axis=1).astype(BF16), rhs)
            y_ref[:, j * LANES:(j + 1) * LANES] = (
                y_intra + y_inter[:, e * S_HEADDIM:(e + 2) * S_HEADDIM]
                + x_pair.astype(F32) * dsk_ref[:, j * LANES:(j + 1) * LANES])
        gl = g % groups_per_half
        xdtw = (xs_halves[g // groups_per_half][0, :, gl * gw:(gl + 1) * gw].astype(F32) * dtw64).astype(BF16)
        st_ref[:, g * gw:(g + 1) * gw] = st_g * ex64[L - 1:L, :] + _dot(bg.astype(F32).T.astype(BF16), xdtw)

    y = y_ref[...] * zs_ref[0].astype(F32)
    out_ref[0] = _rms(y, g_ref[...]).astype(out_ref.dtype)


N_MLSTM_IN, N_SSD_IN = 8, 12


def _mixer_kernel(*refs):
    m_in = refs[:N_MLSTM_IN]
    s_in = refs[N_MLSTM_IN:N_MLSTM_IN + N_SSD_IN]
    ha_ref, yb_ref = refs[N_MLSTM_IN + N_SSD_IN:N_MLSTM_IN + N_SSD_IN + 2]
    c_st, n_st, m_st, cols_ref, rows_ref, st_ref, y_ref = refs[N_MLSTM_IN + N_SSD_IN + 2:]

    @pl.when(pl.program_id(1) == 0)
    def _():
        for ref in (c_st, n_st, m_st, st_ref):
            ref[...] = jnp.zeros_like(ref)

    _mlstm_chunk(*m_in, ha_ref, c_st, n_st, m_st, cols_ref, rows_ref)
    _ssd_chunk(*s_in, yb_ref, st_ref, y_ref)


def _mixers(proj3, misc3, layer, gate_bias, g_head, dtb, alog, dsk, g_out):
    bsz, s_len, _ = proj3.shape
    L = CHUNK
    gn = S_GROUPS * S_STATE
    n_gate = 2 * M_HEADS
    em = _selection_matrix(3, range(n_gate), LANES)
    dt_lanes = range(MISC_DT, MISC_DT + S_HEADS)
    e128 = _selection_matrix(2, dt_lanes, LANES)
    e64 = _selection_matrix(2, dt_lanes, S_HEADDIM)
    out_spec = lambda width: pl.BlockSpec((1, L, width), lambda b, c: (b, c, 0))
    mlstm_specs = [_chunk_spec(M_HEADS * M_QK, OFF_Q), _chunk_spec(M_HEADS * M_QK, OFF_K),
                   _chunk_spec(M_INNER, OFF_V), _chunk_spec(M_INNER, OFF_O), _chunk_spec(LANES, 0),
                   _layer_spec((1, LANES), layer), _layer_spec((M_HEADS, M_V), layer), _const_spec(em.shape)]
    stacked = [dtb, alog, dsk, g_out]
    ssd_specs = ([_chunk_spec(PROJ_HALF, OFF_XS0), _chunk_spec(PROJ_HALF, OFF_XS1), _chunk_spec(S_INNER, OFF_Z),
                  _chunk_spec(gn, OFF_B), _chunk_spec(gn, OFF_C), _chunk_spec(LANES, 0)]
                 + [_layer_spec(a.shape[1:], layer) for a in stacked]
                 + [_const_spec(e128.shape), _const_spec(e64.shape)])
    assert len(mlstm_specs) == N_MLSTM_IN and len(ssd_specs) == N_SSD_IN
    return pl.pallas_call(
        _mixer_kernel,
        grid=(bsz, s_len // L),
        in_specs=mlstm_specs + ssd_specs,
        out_specs=[out_spec(M_INNER), out_spec(S_INNER)],
        out_shape=[jax.ShapeDtypeStruct((bsz, s_len, M_INNER), BF16),
                   jax.ShapeDtypeStruct((bsz, s_len, S_INNER), BF16)],
        scratch_shapes=[pltpu.VMEM((M_HEADS, M_QK, M_V), F32),
                        pltpu.VMEM((SUBLANES, M_QK), F32),
                        pltpu.VMEM((SUBLANES, LANES), F32),
                        pltpu.VMEM((L, n_gate * LANES), F32),
                        pltpu.VMEM((LANES, L), F32),
                        pltpu.VMEM((S_STATE, S_INNER), F32),
                        pltpu.VMEM((L, S_INNER), F32)],
        compiler_params=pltpu.CompilerParams(dimension_semantics=("parallel", "arbitrary"),
                                             vmem_limit_bytes=VMEM_LIMIT),
        name="mixers",
    )(proj3, proj3, proj3, proj3, misc3, gate_bias, g_head, em,
      proj3, proj3, proj3, proj3, proj3, misc3, *stacked, e128, e64)


def _merge_kernel(h_ref, ha_ref, yb_ref, ga_ref, gb_ref, wa_ref, wb_ref, wo_ref, o_ref):
    a = _dot(ha_ref[...], wa_ref[...])
    b = _dot(yb_ref[...], wb_ref[...])
    merged = _sigmoid(ga_ref[...].astype(F32)) * a + _sigmoid(gb_ref[...].astype(F32)) * b
    o_ref[...] = h_ref[...] + _dot(merged.astype(BF16), wo_ref[...])


def _merge(h, ha, yb, proj, layer, wa, wb, wo, tm=512):
    t = h.shape[0]
    tm = min(tm, t)
    row = lambda i: (i, 0)
    return pl.pallas_call(
        _merge_kernel,
        grid=(t // tm,),
        in_specs=[pl.BlockSpec((tm, D_MODEL), row), pl.BlockSpec((tm, M_INNER), row),
                  pl.BlockSpec((tm, S_INNER), row),
                  pl.BlockSpec((tm, D_MODEL), lambda i: (i, OFF_GATE_A // D_MODEL)),
                  pl.BlockSpec((tm, D_MODEL), lambda i: (i, OFF_GATE_B // D_MODEL)),
                  _layer_spec((M_INNER, D_MODEL), layer), _layer_spec((S_INNER, D_MODEL), layer),
                  _layer_spec((D_MODEL, D_MODEL), layer)],
        out_specs=pl.BlockSpec((tm, D_MODEL), row),
        out_shape=jax.ShapeDtypeStruct((t, D_MODEL), F32),
        compiler_params=pltpu.CompilerParams(dimension_semantics=("parallel",),
                                             vmem_limit_bytes=VMEM_LIMIT),
        name="merge",
    )(h, ha, yb, proj, proj, wa, wb, wo)


def _reorder_w_in(w_in):
    sizes = (M_HEADS * M_QK, M_HEADS * M_QK, M_INNER, M_INNER, M_HEADS, M_HEADS, S_INNER,
             S_INNER, S_GROUPS * S_STATE, S_GROUPS * S_STATE, S_HEADS, 2 * D_MODEL)
    cuts = [int(c) for c in np.cumsum(sizes)[:-1]]
    q, k, v, o, ig, fg, z, xs, bm, cm, dt, gate = jnp.split(w_in.astype(BF16), cuts, axis=-1)
    pad = jnp.zeros(w_in.shape[:-1] + (LANES - MISC_DT - S_HEADS,), BF16)
    xs0, xs1 = jnp.split(xs, 2, axis=-1)
    gate_a, gate_b = jnp.split(gate, 2, axis=-1)
    return (jnp.concatenate([xs0, gate_a, xs1, gate_b, bm, cm, v, z, o, q, k], axis=-1),
            jnp.concatenate([ig, fg, dt, pad], axis=-1))


def _misc_row(depth, pieces):
    row = jnp.zeros((depth, LANES), F32)
    for off, val in pieces:
        row = row.at[:, off:off + val.shape[-1]].set(val.astype(F32))
    return row[:, None, :]


def kernel(x, p, g_ffn1, w1_ffn1, w3_ffn1, w2_ffn1, g_mix, w_in, b_igate, b_fgate, g_mlstm_head, conv_w, conv_b, dt_bias, a_log, d_skip, g_ssm_out, w_branch_a, w_branch_b, w_out, g_ffn2, w1_ffn2, w3_ffn2, w2_ffn2, g_ple, w_ple_gate, w_ple, g_final):
    bsz, s_len, _ = x.shape
    depth = w_in.shape[0]
    t = bsz * s_len
    bf = lambda a: a.astype(BF16)
    rows = lambda a: a[:, None, :]
    w1a, w3a, w2a = bf(w1_ffn1), bf(w3_ffn1), bf(w2_ffn1)
    w1b, w3b, w2b = bf(w1_ffn2), bf(w3_ffn2), bf(w2_ffn2)
    w_proj, w_misc = _reorder_w_in(w_in)
    wa, wb, wo = bf(w_branch_a), bf(w_branch_b), bf(w_out)
    wg, wp = bf(w_ple_gate), bf(w_ple)
    gate_bias = _misc_row(depth, [(MISC_I, b_igate), (MISC_F, b_fgate)])
    dtb = _misc_row(depth, [(MISC_DT, dt_bias)])
    alog = _misc_row(depth, [(MISC_DT, a_log)])
    dsk = rows(jnp.repeat(d_skip, S_HEADDIM, axis=-1))
    def conv_cols(a):
        out = jnp.zeros(a.shape[:-1] + (N_PROJ,), F32)
        for off, lo, hi in ((OFF_XS0, 0, PROJ_HALF), (OFF_XS1, PROJ_HALF, S_INNER), (OFF_B, S_INNER, a.shape[-1])):
            out = out.at[..., off:off + hi - lo].set(0.5 * a[..., lo:hi])
        return out

    cw_cols, cb_cols = conv_cols(conv_w), conv_cols(rows(conv_b))
    g1, gm, g2, gp, gs = rows(g_ffn1), rows(g_mix), rows(g_ffn2), rows(g_ple), rows(g_ssm_out)

    h = x.reshape(t, D_MODEL)
    p2 = p.reshape(depth, t, PLE_DIM)
    for i in range(depth):
        h = _ffn(h, i, g1, w1a, w3a, w2a)
        proj, misc = _proj(h, s_len, i, gm, w_proj, w_misc, cw_cols, cb_cols)
        proj3 = proj.reshape(bsz, s_len, N_PROJ)
        misc3 = misc.reshape(bsz, s_len, LANES)
        ha, yb = _mixers(proj3, misc3, i, gate_bias, g_mlstm_head, dtb, alog, dsk, gs)
        h = _merge(h, ha.reshape(t, M_INNER), yb.reshape(t, S_INNER), proj, i, wa, wb, wo)
        h = _ffn(h, i, g2, w1b, w3b, w2b, ple=(p2, gp, wg, wp),
                 g_final=g_final[None, :] if i == depth - 1 else None)
    return h.reshape(bsz, s_len, D_MODEL)
```

```python
import functools

import numpy as np
import jax
import jax.numpy as jnp
from jax import lax
from jax.experimental import pallas as pl
from jax.experimental.pallas import tpu as pltpu

F32 = jnp.float32
BF16 = jnp.bfloat16

D_MODEL = 1024
PLE_DIM = 256
CHUNK = 128
EPS = 1e-6
M_HEADS = 4
M_QK = 128
M_V = 256
M_INNER = M_HEADS * M_V
S_INNER = 2048
S_HEADDIM = 64
S_HEADS = 32
S_GROUPS = 4
S_STATE = 128
S_CONV = 4
S_HEADS_PER_GROUP = S_HEADS // S_GROUPS
D_FF = 2816
LANES = 128
SUBLANES = 8
MXU_DIM = 256
LOG2_E = 1.4426950408889634

PROJ_HALF = 1024
PROJ_TN = 2 * PROJ_HALF
OFF_XS0 = 0
OFF_GATE_A = 1024
OFF_XS1 = 2048
OFF_GATE_B = 3072
OFF_B = 4096
OFF_C = 4608
OFF_V = 5120
OFF_Z = 6144
OFF_O = 8192
OFF_Q = 9216
OFF_K = 9728
N_PROJ = 10240
N_CONV_STEPS = 3
STEP_Z = 3
STEP_O = 4
PROJ_PAD = 16
MISC_I = 0
MISC_F = M_HEADS
MISC_DT = 2 * M_HEADS

VMEM_LIMIT = 56 * 1024 * 1024
FFN_CHUNKS = (4 * MXU_DIM, 4 * MXU_DIM, D_FF - 8 * MXU_DIM)


def _dot(a, b):
    return jnp.dot(a, b, preferred_element_type=F32)


def _dot_nt(a, b):
    return lax.dot_general(a, b, (((1,), (1,)), ((), ())), preferred_element_type=F32)


def _rms(x, g):
    r = lax.rsqrt(jnp.mean(x * x, axis=-1, keepdims=True) + EPS)
    return x * r * g


def _sigmoid(x):
    return 0.5 * jnp.tanh(0.5 * x) + 0.5


def _silu(x):
    half = 0.5 * x
    return half + half * jnp.tanh(half)


def _split3(x):
    hi = x.astype(BF16)
    r = x - hi.astype(F32)
    mid = r.astype(BF16)
    lo = (r - mid.astype(F32)).astype(BF16)
    return hi, mid, lo


def _cumsum_rows(tril_bf16, x):
    hi, mid, lo = _split3(x)
    return _dot(tril_bf16, hi) + _dot(tril_bf16, mid) + _dot(tril_bf16, lo)


def _expand_lanes(x, e_ref, parts=2):
    return _dot(jnp.concatenate(_split3(x)[:parts], axis=1), e_ref[...])


def _selection_matrix(parts, src_lanes, width):
    e = np.zeros((LANES, len(src_lanes) * width), np.float32)
    for c, src in enumerate(src_lanes):
        e[src, c * width:(c + 1) * width] = 1.0
    return jnp.asarray(np.concatenate([e] * parts, axis=0), dtype=BF16)


def _layer_spec(shape, layer):
    nd = len(shape)
    return pl.BlockSpec((None,) + tuple(shape), lambda *_: (layer,) + (0,) * nd,
                        pipeline_mode=pl.Buffered(1))


def _const_spec(shape):
    nd = len(shape)
    return pl.BlockSpec(shape, lambda *_: (0,) * nd, pipeline_mode=pl.Buffered(1))


def _ffn_kernel(*refs, has_ple, final_norm):
    h_ref, g_ref, w1_ref, w3_ref, w2_ref = refs[:5]
    pos = 5
    if has_ple:
        p_ref, gp_ref, wg_ref, wp_ref = refs[pos:pos + 4]
        pos += 4
    if final_norm:
        gf_ref = refs[pos]
        pos += 1
    o_ref = refs[pos]

    h = h_ref[...]
    u = _rms(h, g_ref[...]).astype(BF16)
    acc = None
    lo = 0
    for fc in FFN_CHUNKS:
        a = _dot(u, w1_ref[:, lo:lo + fc])
        b = _dot(u, w3_ref[:, lo:lo + fc])
        hid = (_silu(a) * b).astype(BF16)
        d = _dot(hid, w2_ref[lo:lo + fc, :])
        acc = d if acc is None else acc + d
        lo += fc
    h = h + 0.5 * acc
    if has_ple:
        ug = _rms(h, gp_ref[...]).astype(BF16)
        gate = _sigmoid(_dot(ug, wg_ref[...]))
        pe = _dot(p_ref[...].astype(BF16), wp_ref[...])
        h = h + gate * pe
    if final_norm:
        h = _rms(h, gf_ref[...])
    o_ref[...] = h


def _ffn(h, layer, g, w1, w3, w2, ple=None, g_final=None, tm=512):
    t = h.shape[0]
    tm = min(tm, t)
    row = lambda i: (i, 0)
    in_specs = [pl.BlockSpec((tm, D_MODEL), row), _layer_spec((1, D_MODEL), layer),
                _layer_spec((D_MODEL, D_FF), layer), _layer_spec((D_MODEL, D_FF), layer),
                _layer_spec((D_FF, D_MODEL), layer)]
    args = [h, g, w1, w3, w2]
    if ple is not None:
        p, gp, wg, wp = ple
        in_specs += [pl.BlockSpec((None, tm, PLE_DIM), lambda i: (layer, i, 0)),
                     _layer_spec((1, D_MODEL), layer), _layer_spec((D_MODEL, D_MODEL), layer),
                     _layer_spec((PLE_DIM, D_MODEL), layer)]
        args += [p, gp, wg, wp]
    if g_final is not None:
        in_specs.append(_const_spec((1, D_MODEL)))
        args.append(g_final)
    return pl.pallas_call(
        functools.partial(_ffn_kernel, has_ple=ple is not None, final_norm=g_final is not None),
        grid=(t // tm,),
        in_specs=in_specs,
        out_specs=pl.BlockSpec((tm, D_MODEL), row),
        out_shape=jax.ShapeDtypeStruct((t, D_MODEL), F32),
        compiler_params=pltpu.CompilerParams(dimension_semantics=("parallel",),
                                             vmem_limit_bytes=VMEM_LIMIT),
        name="ffn",
    )(*args)


def _proj_kernel(hp_ref, h_ref, g_ref, w_ref, wm_ref, cw_ref, cb_ref, o_ref, misc_ref, u_ref, *, tiles_per_seq):
    i = pl.program_id(0)
    j = pl.program_id(1)
    tm = o_ref.shape[0]

    @pl.when(j == 0)
    def _():
        u = _rms(h_ref[...], g_ref[...]).astype(BF16)
        u_ref[PROJ_PAD:, :] = u
        up = _rms(hp_ref[...], g_ref[...])
        u_ref[0:PROJ_PAD, :] = jnp.where(i % tiles_per_seq == 0, 0.0, up).astype(BF16)
        misc_ref[...] = _dot(u, wm_ref[...])

    n_sub = PROJ_HALF // MXU_DIM
    sub = lambda half, n: slice(half * PROJ_HALF + n * MXU_DIM, half * PROJ_HALF + (n + 1) * MXU_DIM)

    def conv_sub(cs):
        a = _dot(u_ref[...], w_ref[:, cs])
        a1 = pltpu.roll(a, 1, axis=0)
        b = cw_ref[1:2, cs] * a + cw_ref[0:1, cs] * a1
        c = cw_ref[3:4, cs] * a + cw_ref[2:3, cs] * a1 + cb_ref[:, cs]
        half = (c + pltpu.roll(b, 2, axis=0))[PROJ_PAD:, :]
        o_ref[:, cs] = (half + half * jnp.tanh(half)).astype(o_ref.dtype)

    def plain_sub(cs, fn):
        o_ref[:, cs] = fn(_dot(u_ref[PROJ_PAD:, :], w_ref[:, cs])).astype(o_ref.dtype)

    @pl.when(j < N_CONV_STEPS)
    def _():
        for n in range(n_sub):
            conv_sub(sub(0, n))
            plain_sub(sub(1, n), lambda v: v)

    @pl.when(j == STEP_Z)
    def _():
        for n in range(n_sub):
            plain_sub(sub(0, n), _silu)
            plain_sub(sub(1, n), _silu)

    @pl.when(j == STEP_O)
    def _():
        for n in range(n_sub):
            plain_sub(sub(0, n), _sigmoid)
            plain_sub(sub(1, n), lambda v: v)


def _proj(h, s_len, layer, g, w, w_misc, cw, cb, tm=1024):
    t = h.shape[0]
    tm = min(tm, s_len)
    tn = PROJ_TN
    pads_per_tile = tm // PROJ_PAD
    return pl.pallas_call(
        functools.partial(_proj_kernel, tiles_per_seq=s_len // tm),
        grid=(t // tm, N_PROJ // tn),
        in_specs=[pl.BlockSpec((PROJ_PAD, D_MODEL), lambda i, j: (jnp.maximum(i * pads_per_tile - 1, 0), 0)),
                  pl.BlockSpec((tm, D_MODEL), lambda i, j: (i, 0)),
                  pl.BlockSpec((None, 1, D_MODEL), lambda i, j: (layer, 0, 0)),
                  pl.BlockSpec((None, D_MODEL, tn), lambda i, j: (layer, 0, j)),
                  pl.BlockSpec((None, D_MODEL, LANES), lambda i, j: (layer, 0, 0)),
                  pl.BlockSpec((None, S_CONV, tn), lambda i, j: (layer, 0, j)),
                  pl.BlockSpec((None, 1, tn), lambda i, j: (layer, 0, j))],
        out_specs=[pl.BlockSpec((tm, tn), lambda i, j: (i, j)),
                   pl.BlockSpec((tm, LANES), lambda i, j: (i, 0))],
        out_shape=[jax.ShapeDtypeStruct((t, N_PROJ), BF16), jax.ShapeDtypeStruct((t, LANES), F32)],
        scratch_shapes=[pltpu.VMEM((tm + PROJ_PAD, D_MODEL), BF16)],
        compiler_params=pltpu.CompilerParams(dimension_semantics=("parallel", "arbitrary"),
                                             vmem_limit_bytes=VMEM_LIMIT),
        name="proj",
    )(h, h, g, w, w_misc, cw, cb)


def _mlstm_chunk(q_ref, k_ref, v_ref, misc_ref, bias_ref, em_ref, out_ref,
                 c_st, n_st, m_st, cols_ref, rows_ref):
    L = CHUNK
    scale = M_QK ** -0.5
    wide = lambda a: jnp.concatenate([a] * (M_V // LANES), axis=1)

    x = misc_ref[0] + bias_ref[...]
    lane = lax.broadcasted_iota(jnp.int32, (L, LANES), 1)
    logf = jnp.minimum(x, 0.0) - jnp.log1p(jnp.exp(-jnp.abs(x)))
    xg = jnp.where(lane < MISC_F, x, logf)
    t_idx = lax.broadcasted_iota(jnp.int32, (L, L), 0)
    s_idx = lax.broadcasted_iota(jnp.int32, (L, L), 1)
    causal = s_idx <= t_idx
    tril = jnp.where(causal, 1.0, 0.0).astype(BF16)
    bc = _cumsum_rows(tril, xg)
    comb = jnp.where(lane < MISC_F, xg, bc)
    rows_ref[...] = comb.T
    cols_ref[...] = _expand_lanes(comb, em_ref, parts=3)

    for h in range(M_HEADS):
        qb = q_ref[0, :, h * M_QK:(h + 1) * M_QK]
        kb = k_ref[0, :, h * M_QK:(h + 1) * M_QK]
        vb = v_ref[0, :, h * M_V:(h + 1) * M_V]
        li_col = cols_ref[:, (MISC_I + h) * LANES:(MISC_I + h + 1) * LANES]
        bc_col = cols_ref[:, (MISC_F + h) * LANES:(MISC_F + h + 1) * LANES]
        li_row = rows_ref[MISC_I + h:MISC_I + h + 1, :]
        bc_row = rows_ref[MISC_F + h:MISC_F + h + 1, :]
        gtot = bc_col[L - 1:L, :]
        m_prev = m_st[h:h + 1, :]
        n_prev = n_st[h:h + 1, :]
        c_prev = c_st[h]

        dlog = jnp.where(causal, bc_col - bc_row + li_row, -jnp.inf)
        m_inter = bc_col + m_prev
        m_t = jnp.maximum(jnp.max(dlog, axis=1, keepdims=True), m_inter)
        w_intra = jnp.exp(dlog - m_t)
        w_inter = jnp.exp(m_inter - m_t)

        sc = _dot_nt(qb, kb) * scale * w_intra
        qn = jnp.sum(qb.astype(F32) * n_prev, axis=1, keepdims=True) * scale
        den = jnp.sum(sc, axis=1, keepdims=True) + w_inter * qn
        inv = 1.0 / jnp.maximum(jnp.abs(den), jnp.exp(-m_t))
        num = _dot(sc.astype(BF16), vb) + _dot(qb, c_prev.astype(BF16)) * wide(scale * w_inter)
        out_ref[0, :, h * M_V:(h + 1) * M_V] = (num * wide(inv)).astype(out_ref.dtype)

        a_col = gtot - bc_col + li_col
        m_loc = jnp.max(a_col, axis=0, keepdims=True)
        kw = kb.astype(F32) * jnp.exp(a_col - m_loc)
        c_loc = _dot(kw.T.astype(BF16), vb)
        n_loc = jnp.sum(kw, axis=0, keepdims=True)
        m_new = jnp.maximum(gtot + m_prev, m_loc)
        sp = jnp.exp(gtot + m_prev - m_new)
        sl = jnp.exp(m_loc - m_new)
        c_st[h] = wide(sp) * c_prev + wide(sl) * c_loc
        n_st[h:h + 1, :] = sp * n_prev + sl * n_loc
        m_st[h:h + 1, :] = m_new


def _chunk_spec(width, off):
    return pl.BlockSpec((1, CHUNK, width), lambda b, c: (b, c, off // width))


def _ssd_chunk(xs0_ref, xs1_ref, b_ref, c_ref, misc_ref, dtb_ref, alog_ref, dsk_ref, e128_ref, e64_ref,
               out_ref, st_ref):
    L = CHUNK
    xs_halves = (xs0_ref, xs1_ref)
    pairs_per_half = PROJ_HALF // LANES
    groups_per_half = S_GROUPS // 2

    lane = lax.broadcasted_iota(jnp.int32, (L, LANES), 1)
    dt_lanes = (lane >= MISC_DT) & (lane < MISC_DT + S_HEADS)
    xm = misc_ref[0] + dtb_ref[...]
    dt = jnp.maximum(xm, 0.0) + jnp.log1p(jnp.exp(-jnp.abs(xm)))
    dt = jnp.where(dt_lanes, dt, 0.0)
    a_neg = jnp.where(dt_lanes[0:1, :], -jnp.exp(alog_ref[...]), 0.0)
    t_idx = lax.broadcasted_iota(jnp.int32, (L, L), 0)
    s_idx = lax.broadcasted_iota(jnp.int32, (L, L), 1)
    causal = s_idx <= t_idx
    tril = jnp.where(causal, 1.0, 0.0).astype(BF16)
    acs = _cumsum_rows(tril, dt * a_neg)
    last = acs[L - 1:L, :]
    dtw = dt * jnp.exp(last - acs)
    acs2 = acs * LOG2_E
    adj_t = (acs2 - jnp.where(dt_lanes, jnp.log2(dt), 0.0)).T
    acs_parts = jnp.concatenate(_split3(acs2)[:2], axis=1)
    both_parts = jnp.concatenate(_split3(jnp.concatenate([dtw, jnp.exp(acs)], axis=0))[:2], axis=1)
    lane_lo = lane < S_HEADDIM
    gw = S_HEADS_PER_GROUP * S_HEADDIM

    for g in range(S_GROUPS):
        bg = b_ref[0, :, g * S_STATE:(g + 1) * S_STATE]
        cg = c_ref[0, :, g * S_STATE:(g + 1) * S_STATE]
        cb = _dot_nt(cg, bg)
        cols = _dot(acs_parts, e128_ref[:, g * S_HEADS_PER_GROUP * LANES:(g + 1) * S_HEADS_PER_GROUP * LANES])
        both = _dot(both_parts, e64_ref[:, g * gw:(g + 1) * gw])
        dtw64 = both[0:L, :]
        ex64 = both[L:2 * L, :]
        st_g = st_ref[:, g * gw:(g + 1) * gw]
        y_inter = _dot(cg, st_g.astype(BF16)) * ex64
        for e in range(0, S_HEADS_PER_GROUP, 2):
            h0 = g * S_HEADS_PER_GROUP + e
            j = h0 // 2
            jl = j % pairs_per_half
            x_pair = xs_halves[j // pairs_per_half][0, :, jl * LANES:(jl + 1) * LANES]
            zero = jnp.zeros_like(x_pair)
            rhs = jnp.concatenate([jnp.where(lane_lo, x_pair, zero), jnp.where(lane_lo, zero, x_pair)], axis=0)
            ws = []
            for k in (e, e + 1):
                col = cols[:, k * LANES:(k + 1) * LANES]
                row = adj_t[MISC_DT + h0 - e + k:MISC_DT + h0 - e + k + 1, :]
                ws.append(cb * jnp.exp2(jnp.where(causal, col - row, -jnp.inf)))
            y_intra = _dot(jnp.concatenate(ws, axis=1).astype(BF16), rhs)
            out_ref[0, :, j * LANES:(j + 1) * LANES] = (
                y_intra + y_inter[:, e * S_HEADDIM:(e + 2) * S_HEADDIM]
                + x_pair.astype(F32) * dsk_ref[:, j * LANES:(j + 1) * LANES]).astype(out_ref.dtype)
        gl = g % groups_per_half
        xdtw = (xs_halves[g // groups_per_half][0, :, gl * gw:(gl + 1) * gw].astype(F32) * dtw64).astype(BF16)
        st_ref[:, g * gw:(g + 1) * gw] = st_g * ex64[L - 1:L, :] + _dot(bg.astype(F32).T.astype(BF16), xdtw)


N_MLSTM_IN, N_SSD_IN = 6, 10


def _mixer_kernel(*refs):
    m_in = refs[:N_MLSTM_IN]
    s_in = refs[N_MLSTM_IN:N_MLSTM_IN + N_SSD_IN]
    ha_ref, yb_ref = refs[N_MLSTM_IN + N_SSD_IN:N_MLSTM_IN + N_SSD_IN + 2]
    c_st, n_st, m_st, cols_ref, rows_ref, st_ref = refs[N_MLSTM_IN + N_SSD_IN + 2:]

    @pl.when(pl.program_id(1) == 0)
    def _():
        for ref in (c_st, n_st, m_st, st_ref):
            ref[...] = jnp.zeros_like(ref)

    _mlstm_chunk(*m_in, ha_ref, c_st, n_st, m_st, cols_ref, rows_ref)
    _ssd_chunk(*s_in, yb_ref, st_ref)


def _mixers(proj3, misc3, layer, gate_bias, dtb, alog, dsk):
    bsz, s_len, _ = proj3.shape
    L = CHUNK
    gn = S_GROUPS * S_STATE
    n_gate = 2 * M_HEADS
    em = _selection_matrix(3, range(n_gate), LANES)
    dt_lanes = range(MISC_DT, MISC_DT + S_HEADS)
    e128 = _selection_matrix(2, dt_lanes, LANES)
    e64 = _selection_matrix(2, dt_lanes, S_HEADDIM)
    out_spec = lambda width: pl.BlockSpec((1, L, width), lambda b, c: (b, c, 0))
    mlstm_specs = [_chunk_spec(M_HEADS * M_QK, OFF_Q), _chunk_spec(M_HEADS * M_QK, OFF_K),
                   _chunk_spec(M_INNER, OFF_V), _chunk_spec(LANES, 0),
                   _layer_spec((1, LANES), layer), _const_spec(em.shape)]
    stacked = [dtb, alog, dsk]
    ssd_specs = ([_chunk_spec(PROJ_HALF, OFF_XS0), _chunk_spec(PROJ_HALF, OFF_XS1),
                  _chunk_spec(gn, OFF_B), _chunk_spec(gn, OFF_C), _chunk_spec(LANES, 0)]
                 + [_layer_spec(a.shape[1:], layer) for a in stacked]
                 + [_const_spec(e128.shape), _const_spec(e64.shape)])
    assert len(mlstm_specs) == N_MLSTM_IN and len(ssd_specs) == N_SSD_IN
    return pl.pallas_call(
        _mixer_kernel,
        grid=(bsz, s_len // L),
        in_specs=mlstm_specs + ssd_specs,
        out_specs=[out_spec(M_INNER), out_spec(S_INNER)],
        out_shape=[jax.ShapeDtypeStruct((bsz, s_len, M_INNER), BF16),
                   jax.ShapeDtypeStruct((bsz, s_len, S_INNER), BF16)],
        scratch_shapes=[pltpu.VMEM((M_HEADS, M_QK, M_V), F32),
                        pltpu.VMEM((SUBLANES, M_QK), F32),
                        pltpu.VMEM((SUBLANES, LANES), F32),
                        pltpu.VMEM((L, n_gate * LANES), F32),
                        pltpu.VMEM((LANES, L), F32),
                        pltpu.VMEM((S_STATE, S_INNER), F32)],
        compiler_params=pltpu.CompilerParams(dimension_semantics=("parallel", "arbitrary"),
                                             vmem_limit_bytes=VMEM_LIMIT),
        name="mixers",
    )(proj3, proj3, proj3, misc3, gate_bias, em,
      proj3, proj3, proj3, proj3, misc3, *stacked, e128, e64)


def _merge_kernel(h_ref, hh_ref, y_ref, og_ref, zs_ref, ga_ref, gb_ref, gh_ref, gs_ref, wa_ref, wb_ref, wo_ref,
                  o_ref):
    a = None
    for k in range(M_HEADS):
        cs = slice(k * M_V, (k + 1) * M_V)
        seg = hh_ref[:, cs].astype(F32)
        r = lax.rsqrt(jnp.mean(seg * seg, axis=1, keepdims=True) + EPS)
        ha = (seg * r * gh_ref[:, cs] * og_ref[:, cs].astype(F32)).astype(BF16)
        d = _dot(ha, wa_ref[cs, :])
        a = d if a is None else a + d
    n_chunks = S_INNER // MXU_DIM
    chunks = [slice(c * MXU_DIM, (c + 1) * MXU_DIM) for c in range(n_chunks)]
    ssq = None
    for cs in chunks:
        yz = y_ref[:, cs].astype(F32) * zs_ref[:, cs].astype(F32)
        s = jnp.sum(yz * yz, axis=1, keepdims=True)
        ssq = s if ssq is None else ssq + s
    r = lax.rsqrt(ssq * (1.0 / S_INNER) + EPS)
    b = None
    for cs in chunks:
        yz = y_ref[:, cs].astype(F32) * zs_ref[:, cs].astype(F32)
        d = _dot((yz * r * gs_ref[:, cs]).astype(BF16), wb_ref[cs, :])
        b = d if b is None else b + d
    merged = _sigmoid(ga_ref[...].astype(F32)) * a + _sigmoid(gb_ref[...].astype(F32)) * b
    o_ref[...] = h_ref[...] + _dot(merged.astype(BF16), wo_ref[...])


def _merge(h, hh, y, proj, layer, g_head, g_ssm, wa, wb, wo, tm=512):
    t = h.shape[0]
    tm = min(tm, t)
    row = lambda i: (i, 0)
    return pl.pallas_call(
        _merge_kernel,
        grid=(t // tm,),
        in_specs=[pl.BlockSpec((tm, D_MODEL), row), pl.BlockSpec((tm, M_INNER), row),
                  pl.BlockSpec((tm, S_INNER), row),
                  pl.BlockSpec((tm, M_INNER), lambda i: (i, OFF_O // M_INNER)),
                  pl.BlockSpec((tm, S_INNER), lambda i: (i, OFF_Z // S_INNER)),
                  pl.BlockSpec((tm, D_MODEL), lambda i: (i, OFF_GATE_A // D_MODEL)),
                  pl.BlockSpec((tm, D_MODEL), lambda i: (i, OFF_GATE_B // D_MODEL)),
                  _layer_spec((1, M_INNER), layer), _layer_spec((1, S_INNER), layer),
                  _layer_spec((M_INNER, D_MODEL), layer), _layer_spec((S_INNER, D_MODEL), layer),
                  _layer_spec((D_MODEL, D_MODEL), layer)],
        out_specs=pl.BlockSpec((tm, D_MODEL), row),
        out_shape=jax.ShapeDtypeStruct((t, D_MODEL), F32),
        compiler_params=pltpu.CompilerParams(dimension_semantics=("parallel",),
                                             vmem_limit_bytes=VMEM_LIMIT),
        name="merge",
    )(h, hh, y, proj, proj, proj, proj, g_head, g_ssm, wa, wb, wo)


def _reorder_w_in(w_in):
    sizes = (M_HEADS * M_QK, M_HEADS * M_QK, M_INNER, M_INNER, M_HEADS, M_HEADS, S_INNER,
             S_INNER, S_GROUPS * S_STATE, S_GROUPS * S_STATE, S_HEADS, 2 * D_MODEL)
    cuts = [int(c) for c in np.cumsum(sizes)[:-1]]
    q, k, v, o, ig, fg, z, xs, bm, cm, dt, gate = jnp.split(w_in.astype(BF16), cuts, axis=-1)
    pad = jnp.zeros(w_in.shape[:-1] + (LANES - MISC_DT - S_HEADS,), BF16)
    xs0, xs1 = jnp.split(xs, 2, axis=-1)
    gate_a, gate_b = jnp.split(gate, 2, axis=-1)
    return (jnp.concatenate([xs0, gate_a, xs1, gate_b, bm, cm, v, z, o, q, k], axis=-1),
            jnp.concatenate([ig, fg, dt, pad], axis=-1))


def _misc_row(depth, pieces):
    row = jnp.zeros((depth, LANES), F32)
    for off, val in pieces:
        row = row.at[:, off:off + val.shape[-1]].set(val.astype(F32))
    return row[:, None, :]


def kernel(x, p, g_ffn1, w1_ffn1, w3_ffn1, w2_ffn1, g_mix, w_in, b_igate, b_fgate, g_mlstm_head, conv_w, conv_b, dt_bias, a_log, d_skip, g_ssm_out, w_branch_a, w_branch_b, w_out, g_ffn2, w1_ffn2, w3_ffn2, w2_ffn2, g_ple, w_ple_gate, w_ple, g_final):
    bsz, s_len, _ = x.shape
    depth = w_in.shape[0]
    t = bsz * s_len
    bf = lambda a: a.astype(BF16)
    rows = lambda a: a[:, None, :]
    w1a, w3a, w2a = bf(w1_ffn1), bf(w3_ffn1), bf(w2_ffn1)
    w1b, w3b, w2b = bf(w1_ffn2), bf(w3_ffn2), bf(w2_ffn2)
    w_proj, w_misc = _reorder_w_in(w_in)
    wa, wb, wo = bf(w_branch_a), bf(w_branch_b), bf(w_out)
    wg, wp = bf(w_ple_gate), bf(w_ple)
    gate_bias = _misc_row(depth, [(MISC_I, b_igate), (MISC_F, b_fgate)])
    dtb = _misc_row(depth, [(MISC_DT, dt_bias)])
    alog = _misc_row(depth, [(MISC_DT, a_log)])
    dsk = rows(jnp.repeat(d_skip, S_HEADDIM, axis=-1))

    def conv_cols(a):
        out = jnp.zeros(a.shape[:-1] + (N_PROJ,), F32)
        for off, lo, hi in ((OFF_XS0, 0, PROJ_HALF), (OFF_XS1, PROJ_HALF, S_INNER), (OFF_B, S_INNER, a.shape[-1])):
            out = out.at[..., off:off + hi - lo].set(0.5 * a[..., lo:hi])
        return out

    cw_cols, cb_cols = conv_cols(conv_w), conv_cols(rows(conv_b))
    g_head = g_mlstm_head.reshape(depth, 1, M_INNER)
    g1, gm, g2, gp, gs = rows(g_ffn1), rows(g_mix), rows(g_ffn2), rows(g_ple), rows(g_ssm_out)

    h = x.reshape(t, D_MODEL)
    p2 = p.reshape(depth, t, PLE_DIM)
    for i in range(depth):
        h = _ffn(h, i, g1, w1a, w3a, w2a)
        proj, misc = _proj(h, s_len, i, gm, w_proj, w_misc, cw_cols, cb_cols)
        proj3 = proj.reshape(bsz, s_len, N_PROJ)
        misc3 = misc.reshape(bsz, s_len, LANES)
        hh, y = _mixers(proj3, misc3, i, gate_bias, dtb, alog, dsk)
        h = _merge(h, hh.reshape(t, M_INNER), y.reshape(t, S_INNER), proj, i, g_head, gs, wa, wb, wo)
        h = _ffn(h, i, g2, w1b, w3b, w2b, ple=(p2, gp, wg, wp),
                 g_final=g_final[None, :] if i == depth - 1 else None)
    return h.reshape(bsz, s_len, D_MODEL)
```

```python
import functools

import numpy as np
import jax
import jax.numpy as jnp
from jax import lax
from jax.experimental import pallas as pl
from jax.experimental.pallas import tpu as pltpu

F32 = jnp.float32
BF16 = jnp.bfloat16

D_MODEL = 1024
PLE_DIM = 256
CHUNK = 128
EPS = 1e-6
M_HEADS = 4
M_QK = 128
M_V = 256
M_INNER = M_HEADS * M_V
S_INNER = 2048
S_HEADDIM = 64
S_HEADS = 32
S_GROUPS = 4
S_STATE = 128
S_CONV = 4
S_HEADS_PER_GROUP = S_HEADS // S_GROUPS
D_FF = 2816
LANES = 128
SUBLANES = 8
MXU_DIM = 256
LOG2_E = 1.4426950408889634

PROJ_HALF = 1024
PROJ_TN = 2 * PROJ_HALF
OFF_XS0 = 0
OFF_GATE_A = 1024
OFF_XS1 = 2048
OFF_GATE_B = 3072
OFF_B = 4096
OFF_C = 4608
OFF_V = 5120
OFF_Z = 6144
OFF_O = 8192
OFF_Q = 9216
OFF_K = 9728
N_PROJ = 10240
N_CONV_STEPS = 3
STEP_Z = 3
STEP_O = 4
PROJ_PAD = 16
MISC_I = 0
MISC_F = M_HEADS
MISC_DT = 2 * M_HEADS

VMEM_LIMIT = 56 * 1024 * 1024
FFN_CHUNKS = (4 * MXU_DIM, 4 * MXU_DIM, D_FF - 8 * MXU_DIM)


def _dot(a, b):
    return jnp.dot(a, b, preferred_element_type=F32)


def _dot_nt(a, b):
    return lax.dot_general(a, b, (((1,), (1,)), ((), ())), preferred_element_type=F32)


def _rms(x, g):
    r = lax.rsqrt(jnp.mean(x * x, axis=-1, keepdims=True) + EPS)
    return x * r * g


def _sigmoid(x):
    return 0.5 * jnp.tanh(0.5 * x) + 0.5


def _silu(x):
    half = 0.5 * x
    return half + half * jnp.tanh(half)


def _split3(x):
    hi = x.astype(BF16)
    r = x - hi.astype(F32)
    mid = r.astype(BF16)
    lo = (r - mid.astype(F32)).astype(BF16)
    return hi, mid, lo


def _cumsum_rows(tril_bf16, x):
    hi, mid, lo = _split3(x)
    return _dot(tril_bf16, hi) + _dot(tril_bf16, mid) + _dot(tril_bf16, lo)


def _expand_lanes(x, e_ref, parts=2):
    return _dot(jnp.concatenate(_split3(x)[:parts], axis=1), e_ref[...])


def _selection_matrix(parts, src_lanes, width):
    e = np.zeros((LANES, len(src_lanes) * width), np.float32)
    for c, src in enumerate(src_lanes):
        e[src, c * width:(c + 1) * width] = 1.0
    return jnp.asarray(np.concatenate([e] * parts, axis=0), dtype=BF16)


def _layer_spec(shape, layer):
    nd = len(shape)
    return pl.BlockSpec((None,) + tuple(shape), lambda *_: (layer,) + (0,) * nd,
                        pipeline_mode=pl.Buffered(1))


def _const_spec(shape):
    nd = len(shape)
    return pl.BlockSpec(shape, lambda *_: (0,) * nd, pipeline_mode=pl.Buffered(1))


def _ffn_kernel(*refs, has_ple, final_norm):
    h_ref, g_ref, w1_ref, w3_ref, w2_ref = refs[:5]
    pos = 5
    if has_ple:
        p_ref, gp_ref, wg_ref, wp_ref = refs[pos:pos + 4]
        pos += 4
    if final_norm:
        gf_ref = refs[pos]
        pos += 1
    o_ref = refs[pos]

    h = h_ref[...]
    u = _rms(h, g_ref[...]).astype(BF16)
    acc = None
    lo = 0
    for fc in FFN_CHUNKS:
        a = _dot(u, w1_ref[:, lo:lo + fc])
        b = _dot(u, w3_ref[:, lo:lo + fc])
        hid = (_silu(a) * b).astype(BF16)
        d = _dot(hid, w2_ref[lo:lo + fc, :])
        acc = d if acc is None else acc + d
        lo += fc
    h = h + 0.5 * acc
    if has_ple:
        ug = _rms(h, gp_ref[...]).astype(BF16)
        gate = _sigmoid(_dot(ug, wg_ref[...]))
        pe = _dot(p_ref[...].astype(BF16), wp_ref[...])
        h = h + gate * pe
    if final_norm:
        h = _rms(h, gf_ref[...])
    o_ref[...] = h


def _ffn(h, layer, g, w1, w3, w2, ple=None, g_final=None, tm=512):
    t = h.shape[0]
    tm = min(tm, t)
    row = lambda i: (i, 0)
    in_specs = [pl.BlockSpec((tm, D_MODEL), row), _layer_spec((1, D_MODEL), layer),
                _layer_spec((D_MODEL, D_FF), layer), _layer_spec((D_MODEL, D_FF), layer),
                _layer_spec((D_FF, D_MODEL), layer)]
    args = [h, g, w1, w3, w2]
    if ple is not None:
        p, gp, wg, wp = ple
        in_specs += [pl.BlockSpec((None, tm, PLE_DIM), lambda i: (layer, i, 0)),
                     _layer_spec((1, D_MODEL), layer), _layer_spec((D_MODEL, D_MODEL), layer),
                     _layer_spec((PLE_DIM, D_MODEL), layer)]
        args += [p, gp, wg, wp]
    if g_final is not None:
        in_specs.append(_const_spec((1, D_MODEL)))
        args.append(g_final)
    return pl.pallas_call(
        functools.partial(_ffn_kernel, has_ple=ple is not None, final_norm=g_final is not None),
        grid=(t // tm,),
        in_specs=in_specs,
        out_specs=pl.BlockSpec((tm, D_MODEL), row),
        out_shape=jax.ShapeDtypeStruct((t, D_MODEL), F32),
        compiler_params=pltpu.CompilerParams(dimension_semantics=("parallel",),
                                             vmem_limit_bytes=VMEM_LIMIT),
        name="ffn",
    )(*args)


def _proj_kernel(hp_ref, h_ref, g_ref, w_ref, wm_ref, cw_ref, cb_ref, o_ref, misc_ref, u_ref, *, tiles_per_seq):
    i = pl.program_id(0)
    j = pl.program_id(1)
    tm = o_ref.shape[0]

    @pl.when(j == 0)
    def _():
        u = _rms(h_ref[...], g_ref[...]).astype(BF16)
        u_ref[PROJ_PAD:, :] = u
        up = _rms(hp_ref[...], g_ref[...])
        u_ref[0:PROJ_PAD, :] = jnp.where(i % tiles_per_seq == 0, 0.0, up).astype(BF16)
        misc_ref[...] = _dot(u, wm_ref[...])

    n_sub = PROJ_HALF // MXU_DIM
    sub = lambda half, n: slice(half * PROJ_HALF + n * MXU_DIM, half * PROJ_HALF + (n + 1) * MXU_DIM)

    def conv_sub(cs):
        a = _dot(u_ref[...], w_ref[:, cs])
        a1 = pltpu.roll(a, 1, axis=0)
        b = cw_ref[1:2, cs] * a + cw_ref[0:1, cs] * a1
        c = cw_ref[3:4, cs] * a + cw_ref[2:3, cs] * a1 + cb_ref[:, cs]
        half = (c + pltpu.roll(b, 2, axis=0))[PROJ_PAD:, :]
        o_ref[:, cs] = (half + half * jnp.tanh(half)).astype(o_ref.dtype)

    def plain_sub(cs, fn):
        o_ref[:, cs] = fn(_dot(u_ref[PROJ_PAD:, :], w_ref[:, cs])).astype(o_ref.dtype)

    @pl.when(j < N_CONV_STEPS)
    def _():
        for n in range(n_sub):
            conv_sub(sub(0, n))
            plain_sub(sub(1, n), lambda v: v)

    @pl.when(j == STEP_Z)
    def _():
        for n in range(n_sub):
            plain_sub(sub(0, n), _silu)
            plain_sub(sub(1, n), _silu)

    @pl.when(j == STEP_O)
    def _():
        for n in range(n_sub):
            plain_sub(sub(0, n), _sigmoid)
            plain_sub(sub(1, n), lambda v: v)


def _proj(h, s_len, layer, g, w, w_misc, cw, cb, tm=1024):
    t = h.shape[0]
    tm = min(tm, s_len)
    tn = PROJ_TN
    pads_per_tile = tm // PROJ_PAD
    return pl.pallas_call(
        functools.partial(_proj_kernel, tiles_per_seq=s_len // tm),
        grid=(t // tm, N_PROJ // tn),
        in_specs=[pl.BlockSpec((PROJ_PAD, D_MODEL), lambda i, j: (jnp.maximum(i * pads_per_tile - 1, 0), 0)),
                  pl.BlockSpec((tm, D_MODEL), lambda i, j: (i, 0)),
                  pl.BlockSpec((None, 1, D_MODEL), lambda i, j: (layer, 0, 0)),
                  pl.BlockSpec((None, D_MODEL, tn), lambda i, j: (layer, 0, j)),
                  pl.BlockSpec((None, D_MODEL, LANES), lambda i, j: (layer, 0, 0)),
                  pl.BlockSpec((None, S_CONV, tn), lambda i, j: (layer, 0, j)),
                  pl.BlockSpec((None, 1, tn), lambda i, j: (layer, 0, j))],
        out_specs=[pl.BlockSpec((tm, tn), lambda i, j: (i, j)),
                   pl.BlockSpec((tm, LANES), lambda i, j: (i, 0))],
        out_shape=[jax.ShapeDtypeStruct((t, N_PROJ), BF16), jax.ShapeDtypeStruct((t, LANES), F32)],
        scratch_shapes=[pltpu.VMEM((tm + PROJ_PAD, D_MODEL), BF16)],
        compiler_params=pltpu.CompilerParams(dimension_semantics=("parallel", "arbitrary"),
                                             vmem_limit_bytes=VMEM_LIMIT),
        name="proj",
    )(h, h, g, w, w_misc, cw, cb)


def _mlstm_chunk(q_ref, k_ref, v_ref, misc_ref, bias_ref, em_ref, out_ref,
                 c_st, n_st, m_st, cols_ref, rows_ref):
    L = CHUNK
    scale = M_QK ** -0.5
    wide = lambda a: jnp.concatenate([a] * (M_V // LANES), axis=1)

    x = misc_ref[0] + bias_ref[...]
    lane = lax.broadcasted_iota(jnp.int32, (L, LANES), 1)
    logf = jnp.minimum(x, 0.0) - jnp.log1p(jnp.exp(-jnp.abs(x)))
    xg = jnp.where(lane < MISC_F, x, logf)
    t_idx = lax.broadcasted_iota(jnp.int32, (L, L), 0)
    s_idx = lax.broadcasted_iota(jnp.int32, (L, L), 1)
    causal = s_idx <= t_idx
    tril = jnp.where(causal, 1.0, 0.0).astype(BF16)
    bc = _cumsum_rows(tril, xg)
    comb = jnp.where(lane < MISC_F, xg, bc)
    rows_ref[...] = comb.T
    cols_ref[...] = _expand_lanes(comb, em_ref, parts=3)

    for h in range(M_HEADS):
        qb = q_ref[0, :, h * M_QK:(h + 1) * M_QK]
        kb = k_ref[0, :, h * M_QK:(h + 1) * M_QK]
        vb = v_ref[0, :, h * M_V:(h + 1) * M_V]
        li_col = cols_ref[:, (MISC_I + h) * LANES:(MISC_I + h + 1) * LANES]
        bc_col = cols_ref[:, (MISC_F + h) * LANES:(MISC_F + h + 1) * LANES]
        li_row = rows_ref[MISC_I + h:MISC_I + h + 1, :]
        bc_row = rows_ref[MISC_F + h:MISC_F + h + 1, :]
        gtot = bc_col[L - 1:L, :]
        m_prev = m_st[h:h + 1, :]
        n_prev = n_st[h:h + 1, :]
        c_prev = c_st[h]

        dlog = jnp.where(causal, bc_col - bc_row + li_row, -jnp.inf)
        m_inter = bc_col + m_prev
        m_t = jnp.maximum(jnp.max(dlog, axis=1, keepdims=True), m_inter)
        w_intra = jnp.exp(dlog - m_t)
        w_inter = jnp.exp(m_inter - m_t)

        sc = _dot_nt(qb, kb) * scale * w_intra
        qn = jnp.sum(qb.astype(F32) * n_prev, axis=1, keepdims=True) * scale
        den = jnp.sum(sc, axis=1, keepdims=True) + w_inter * qn
        inv = 1.0 / jnp.maximum(jnp.abs(den), jnp.exp(-m_t))
        num = _dot(sc.astype(BF16), vb) + _dot(qb, c_prev.astype(BF16)) * wide(scale * w_inter)
        out_ref[0, :, h * M_V:(h + 1) * M_V] = (num * wide(inv)).astype(out_ref.dtype)

        a_col = gtot - bc_col + li_col
        m_loc = jnp.max(a_col, axis=0, keepdims=True)
        kw = kb.astype(F32) * jnp.exp(a_col - m_loc)
        c_loc = _dot(kw.T.astype(BF16), vb)
        n_loc = jnp.sum(kw, axis=0, keepdims=True)
        m_new = jnp.maximum(gtot + m_prev, m_loc)
        sp = jnp.exp(gtot + m_prev - m_new)
        sl = jnp.exp(m_loc - m_new)
        c_st[h] = wide(sp) * c_prev + wide(sl) * c_loc
        n_st[h:h + 1, :] = sp * n_prev + sl * n_loc
        m_st[h:h + 1, :] = m_new


def _chunk_spec(width, off, seqs=1):
    return pl.BlockSpec((seqs, CHUNK, width), lambda b, c: (b, c, off // width))


def _ssd_chunk(xs0_ref, xs1_ref, b_ref, c_ref, misc_ref, dtb_ref, alog_ref, dsk_ref, e128_ref, e64_ref,
               out_ref, st_ref):
    L = CHUNK
    xs_halves = (xs0_ref, xs1_ref)
    pairs_per_half = PROJ_HALF // LANES
    groups_per_half = S_GROUPS // 2

    lane = lax.broadcasted_iota(jnp.int32, (L, LANES), 1)
    dt_lanes = (lane >= MISC_DT) & (lane < MISC_DT + S_HEADS)
    xm = misc_ref[0] + dtb_ref[...]
    dt = jnp.maximum(xm, 0.0) + jnp.log1p(jnp.exp(-jnp.abs(xm)))
    dt = jnp.where(dt_lanes, dt, 0.0)
    a_neg = jnp.where(dt_lanes[0:1, :], -jnp.exp(alog_ref[...]), 0.0)
    t_idx = lax.broadcasted_iota(jnp.int32, (L, L), 0)
    s_idx = lax.broadcasted_iota(jnp.int32, (L, L), 1)
    causal = s_idx <= t_idx
    tril = jnp.where(causal, 1.0, 0.0).astype(BF16)
    acs = _cumsum_rows(tril, dt * a_neg)
    last = acs[L - 1:L, :]
    dtw = dt * jnp.exp(last - acs)
    acs2 = acs * LOG2_E
    adj_t = (acs2 - jnp.where(dt_lanes, jnp.log2(dt), 0.0)).T
    acs_parts = jnp.concatenate(_split3(acs2)[:2], axis=1)
    both_parts = jnp.concatenate(_split3(jnp.concatenate([dtw, jnp.exp(acs)], axis=0))[:2], axis=1)
    lane_lo = lane < S_HEADDIM
    gw = S_HEADS_PER_GROUP * S_HEADDIM

    for g in range(S_GROUPS):
        bg = b_ref[0, :, g * S_STATE:(g + 1) * S_STATE]
        cg = c_ref[0, :, g * S_STATE:(g + 1) * S_STATE]
        cb = _dot_nt(cg, bg)
        cols = _dot(acs_parts, e128_ref[:, g * S_HEADS_PER_GROUP * LANES:(g + 1) * S_HEADS_PER_GROUP * LANES])
        both = _dot(both_parts, e64_ref[:, g * gw:(g + 1) * gw])
        dtw64 = both[0:L, :]
        ex64 = both[L:2 * L, :]
        st_g = st_ref[:, g * gw:(g + 1) * gw]
        y_inter = _dot(cg, st_g.astype(BF16)) * ex64
        for e in range(0, S_HEADS_PER_GROUP, 2):
            h0 = g * S_HEADS_PER_GROUP + e
            j = h0 // 2
            jl = j % pairs_per_half
            x_pair = xs_halves[j // pairs_per_half][0, :, jl * LANES:(jl + 1) * LANES]
            zero = jnp.zeros_like(x_pair)
            rhs = jnp.concatenate([jnp.where(lane_lo, x_pair, zero), jnp.where(lane_lo, zero, x_pair)], axis=0)
            ws = []
            for k in (e, e + 1):
                col = cols[:, k * LANES:(k + 1) * LANES]
                row = adj_t[MISC_DT + h0 - e + k:MISC_DT + h0 - e + k + 1, :]
                ws.append(cb * jnp.exp2(jnp.where(causal, col - row, -jnp.inf)))
            y_intra = _dot(jnp.concatenate(ws, axis=1).astype(BF16), rhs)
            out_ref[0, :, j * LANES:(j + 1) * LANES] = (
                y_intra + y_inter[:, e * S_HEADDIM:(e + 2) * S_HEADDIM]
                + x_pair.astype(F32) * dsk_ref[:, j * LANES:(j + 1) * LANES]).astype(out_ref.dtype)
        gl = g % groups_per_half
        xdtw = (xs_halves[g // groups_per_half][0, :, gl * gw:(gl + 1) * gw].astype(F32) * dtw64).astype(BF16)
        st_ref[:, g * gw:(g + 1) * gw] = st_g * ex64[L - 1:L, :] + _dot(bg.astype(F32).T.astype(BF16), xdtw)


N_MLSTM_IN, N_SSD_IN = 6, 10
N_MLSTM_BLOCKED, N_SSD_BLOCKED = 4, 5
MIXER_SEQS = 4


def _mixer_kernel(*refs):
    m_in = refs[:N_MLSTM_IN]
    s_in = refs[N_MLSTM_IN:N_MLSTM_IN + N_SSD_IN]
    ha_ref, yb_ref = refs[N_MLSTM_IN + N_SSD_IN:N_MLSTM_IN + N_SSD_IN + 2]
    c_st, n_st, m_st, cols_ref, rows_ref, st_ref = refs[N_MLSTM_IN + N_SSD_IN + 2:]

    @pl.when(pl.program_id(1) == 0)
    def _():
        for ref in (c_st, n_st, m_st, st_ref):
            ref[...] = jnp.zeros_like(ref)

    for bb in range(MIXER_SEQS):
        seq = lambda r: r.at[bb:bb + 1]
        own = lambda r: r.at[bb]
        _mlstm_chunk(*[seq(r) for r in m_in[:N_MLSTM_BLOCKED]], *m_in[N_MLSTM_BLOCKED:], seq(ha_ref),
                     own(c_st), own(n_st), own(m_st), own(cols_ref), own(rows_ref))
        _ssd_chunk(*[seq(r) for r in s_in[:N_SSD_BLOCKED]], *s_in[N_SSD_BLOCKED:], seq(yb_ref), own(st_ref))


def _mixers(proj3, misc3, layer, gate_bias, dtb, alog, dsk):
    bsz, s_len, _ = proj3.shape
    L = CHUNK
    gn = S_GROUPS * S_STATE
    n_gate = 2 * M_HEADS
    seqs = MIXER_SEQS
    assert bsz % seqs == 0
    em = _selection_matrix(3, range(n_gate), LANES)
    dt_lanes = range(MISC_DT, MISC_DT + S_HEADS)
    e128 = _selection_matrix(2, dt_lanes, LANES)
    e64 = _selection_matrix(2, dt_lanes, S_HEADDIM)
    out_spec = lambda width: pl.BlockSpec((seqs, L, width), lambda b, c: (b, c, 0))
    chunk = functools.partial(_chunk_spec, seqs=seqs)
    mlstm_specs = [chunk(M_HEADS * M_QK, OFF_Q), chunk(M_HEADS * M_QK, OFF_K),
                   chunk(M_INNER, OFF_V), chunk(LANES, 0),
                   _layer_spec((1, LANES), layer), _const_spec(em.shape)]
    stacked = [dtb, alog, dsk]
    ssd_specs = ([chunk(PROJ_HALF, OFF_XS0), chunk(PROJ_HALF, OFF_XS1),
                  chunk(gn, OFF_B), chunk(gn, OFF_C), chunk(LANES, 0)]
                 + [_layer_spec(a.shape[1:], layer) for a in stacked]
                 + [_const_spec(e128.shape), _const_spec(e64.shape)])
    assert len(mlstm_specs) == N_MLSTM_IN and len(ssd_specs) == N_SSD_IN
    return pl.pallas_call(
        _mixer_kernel,
        grid=(bsz // seqs, s_len // L),
        in_specs=mlstm_specs + ssd_specs,
        out_specs=[out_spec(M_INNER), out_spec(S_INNER)],
        out_shape=[jax.ShapeDtypeStruct((bsz, s_len, M_INNER), BF16),
                   jax.ShapeDtypeStruct((bsz, s_len, S_INNER), BF16)],
        scratch_shapes=[pltpu.VMEM((seqs, M_HEADS, M_QK, M_V), F32),
                        pltpu.VMEM((seqs, SUBLANES, M_QK), F32),
                        pltpu.VMEM((seqs, SUBLANES, LANES), F32),
                        pltpu.VMEM((seqs, L, n_gate * LANES), F32),
                        pltpu.VMEM((seqs, LANES, L), F32),
                        pltpu.VMEM((seqs, S_STATE, S_INNER), F32)],
        compiler_params=pltpu.CompilerParams(dimension_semantics=("parallel", "arbitrary"),
                                             vmem_limit_bytes=VMEM_LIMIT),
        name="mixers",
    )(proj3, proj3, proj3, misc3, gate_bias, em,
      proj3, proj3, proj3, proj3, misc3, *stacked, e128, e64)


def _merge_kernel(h_ref, hh_ref, y_ref, og_ref, zs_ref, ga_ref, gb_ref, gh_ref, gs_ref, wa_ref, wb_ref, wo_ref,
                  o_ref):
    a = None
    for k in range(M_HEADS):
        cs = slice(k * M_V, (k + 1) * M_V)
        seg = hh_ref[:, cs].astype(F32)
        r = lax.rsqrt(jnp.mean(seg * seg, axis=1, keepdims=True) + EPS)
        ha = (seg * r * gh_ref[:, cs] * og_ref[:, cs].astype(F32)).astype(BF16)
        d = _dot(ha, wa_ref[cs, :])
        a = d if a is None else a + d
    n_chunks = S_INNER // MXU_DIM
    chunks = [slice(c * MXU_DIM, (c + 1) * MXU_DIM) for c in range(n_chunks)]
    ssq = None
    for cs in chunks:
        yz = y_ref[:, cs].astype(F32) * zs_ref[:, cs].astype(F32)
        s = jnp.sum(yz * yz, axis=1, keepdims=True)
        ssq = s if ssq is None else ssq + s
    r = lax.rsqrt(ssq * (1.0 / S_INNER) + EPS)
    b = None
    for cs in chunks:
        yz = y_ref[:, cs].astype(F32) * zs_ref[:, cs].astype(F32)
        d = _dot((yz * r * gs_ref[:, cs]).astype(BF16), wb_ref[cs, :])
        b = d if b is None else b + d
    merged = _sigmoid(ga_ref[...].astype(F32)) * a + _sigmoid(gb_ref[...].astype(F32)) * b
    o_ref[...] = h_ref[...] + _dot(merged.astype(BF16), wo_ref[...])


def _merge(h, hh, y, proj, layer, g_head, g_ssm, wa, wb, wo, tm=512):
    t = h.shape[0]
    tm = min(tm, t)
    row = lambda i: (i, 0)
    return pl.pallas_call(
        _merge_kernel,
        grid=(t // tm,),
        in_specs=[pl.BlockSpec((tm, D_MODEL), row), pl.BlockSpec((tm, M_INNER), row),
                  pl.BlockSpec((tm, S_INNER), row),
                  pl.BlockSpec((tm, M_INNER), lambda i: (i, OFF_O // M_INNER)),
                  pl.BlockSpec((tm, S_INNER), lambda i: (i, OFF_Z // S_INNER)),
                  pl.BlockSpec((tm, D_MODEL), lambda i: (i, OFF_GATE_A // D_MODEL)),
                  pl.BlockSpec((tm, D_MODEL), lambda i: (i, OFF_GATE_B // D_MODEL)),
                  _layer_spec((1, M_INNER), layer), _layer_spec((1, S_INNER), layer),
                  _layer_spec((M_INNER, D_MODEL), layer), _layer_spec((S_INNER, D_MODEL), layer),
                  _layer_spec((D_MODEL, D_MODEL), layer)],
        out_specs=pl.BlockSpec((tm, D_MODEL), row),
        out_shape=jax.ShapeDtypeStruct((t, D_MODEL), F32),
        compiler_params=pltpu.CompilerParams(dimension_semantics=("parallel",),
                                             vmem_limit_bytes=VMEM_LIMIT),
        name="merge",
    )(h, hh, y, proj, proj, proj, proj, g_head, g_ssm, wa, wb, wo)


def _reorder_w_in(w_in):
    sizes = (M_HEADS * M_QK, M_HEADS * M_QK, M_INNER, M_INNER, M_HEADS, M_HEADS, S_INNER,
             S_INNER, S_GROUPS * S_STATE, S_GROUPS * S_STATE, S_HEADS, 2 * D_MODEL)
    cuts = [int(c) for c in np.cumsum(sizes)[:-1]]
    q, k, v, o, ig, fg, z, xs, bm, cm, dt, gate = jnp.split(w_in.astype(BF16), cuts, axis=-1)
    pad = jnp.zeros(w_in.shape[:-1] + (LANES - MISC_DT - S_HEADS,), BF16)
    xs0, xs1 = jnp.split(xs, 2, axis=-1)
    gate_a, gate_b = jnp.split(gate, 2, axis=-1)
    return (jnp.concatenate([xs0, gate_a, xs1, gate_b, bm, cm, v, z, o, q, k], axis=-1),
            jnp.concatenate([ig, fg, dt, pad], axis=-1))


def _misc_row(depth, pieces):
    row = jnp.zeros((depth, LANES), F32)
    for off, val in pieces:
        row = row.at[:, off:off + val.shape[-1]].set(val.astype(F32))
    return row[:, None, :]


def kernel(x, p, g_ffn1, w1_ffn1, w3_ffn1, w2_ffn1, g_mix, w_in, b_igate, b_fgate, g_mlstm_head, conv_w, conv_b, dt_bias, a_log, d_skip, g_ssm_out, w_branch_a, w_branch_b, w_out, g_ffn2, w1_ffn2, w3_ffn2, w2_ffn2, g_ple, w_ple_gate, w_ple, g_final):
    bsz, s_len, _ = x.shape
    depth = w_in.shape[0]
    t = bsz * s_len
    bf = lambda a: a.astype(BF16)
    rows = lambda a: a[:, None, :]
    w1a, w3a, w2a = bf(w1_ffn1), bf(w3_ffn1), bf(w2_ffn1)
    w1b, w3b, w2b = bf(w1_ffn2), bf(w3_ffn2), bf(w2_ffn2)
    w_proj, w_misc = _reorder_w_in(w_in)
    wa, wb, wo = bf(w_branch_a), bf(w_branch_b), bf(w_out)
    wg, wp = bf(w_ple_gate), bf(w_ple)
    gate_bias = _misc_row(depth, [(MISC_I, b_igate), (MISC_F, b_fgate)])
    dtb = _misc_row(depth, [(MISC_DT, dt_bias)])
    alog = _misc_row(depth, [(MISC_DT, a_log)])
    dsk = rows(jnp.repeat(d_skip, S_HEADDIM, axis=-1))

    def conv_cols(a):
        out = jnp.zeros(a.shape[:-1] + (N_PROJ,), F32)
        for off, lo, hi in ((OFF_XS0, 0, PROJ_HALF), (OFF_XS1, PROJ_HALF, S_INNER), (OFF_B, S_INNER, a.shape[-1])):
            out = out.at[..., off:off + hi - lo].set(0.5 * a[..., lo:hi])
        return out

    cw_cols, cb_cols = conv_cols(conv_w), conv_cols(rows(conv_b))
    g_head = g_mlstm_head.reshape(depth, 1, M_INNER)
    g1, gm, g2, gp, gs = rows(g_ffn1), rows(g_mix), rows(g_ffn2), rows(g_ple), rows(g_ssm_out)

    h = x.reshape(t, D_MODEL)
    p2 = p.reshape(depth, t, PLE_DIM)
    for i in range(depth):
        h = _ffn(h, i, g1, w1a, w3a, w2a)
        proj, misc = _proj(h, s_len, i, gm, w_proj, w_misc, cw_cols, cb_cols)
        proj3 = proj.reshape(bsz, s_len, N_PROJ)
        misc3 = misc.reshape(bsz, s_len, LANES)
        hh, y = _mixers(proj3, misc3, i, gate_bias, dtb, alog, dsk)
        h = _merge(h, hh.reshape(t, M_INNER), y.reshape(t, S_INNER), proj, i, g_head, gs, wa, wb, wo)
        h = _ffn(h, i, g2, w1b, w3b, w2b, ple=(p2, gp, wg, wp),
                 g_final=g_final[None, :] if i == depth - 1 else None)
    return h.reshape(bsz, s_len, D_MODEL)
```
